```python
import jax, jax.numpy as jnp
from jax import lax
import numpy as np

D_MODEL = 4096
BATCH = 4
SEQ = 2048
DEPTH = 1
DEC_BATCH = 128
DEC_SEQ = 4
PAST_LEN = 16384
PAGE_SIZE = 128

R_WIDTH = D_MODEL // 2
R_HEAD = 64
R_HEADS = R_WIDTH // R_HEAD
R_DECAY_LORA = 64
R_AAA_LORA = 64
R_GATE_LORA = 256
R_COLS = 3 * R_WIDTH + R_DECAY_LORA + R_AAA_LORA + R_GATE_LORA
RWKV_GN_EPS = 64e-5
G_WIDTH = D_MODEL // 2
G_HEAD = 128
G_HEADS = G_WIDTH // G_HEAD
G_CONV = 4
G_CHUNK = 64
G_CONV_COLS = 3 * G_WIDTH
G_COLS = G_CONV_COLS + G_WIDTH + 2 * G_HEADS
MERGE_COLS = 2 * D_MODEL
IN_COLS = R_COLS + G_COLS + MERGE_COLS
D_FF = 4 * D_MODEL
NORM_EPS = 1e-6

kernel_name = "rwkv7_gdn_parallel_gated_adaln_decoder_step"

R_SPLITS = (R_WIDTH, 2 * R_WIDTH, 3 * R_WIDTH, 3 * R_WIDTH + R_DECAY_LORA,
            3 * R_WIDTH + R_DECAY_LORA + R_AAA_LORA)
G_SPLITS = (G_CONV_COLS, G_CONV_COLS + G_WIDTH, G_CONV_COLS + G_WIDTH + G_HEADS)


def rms_norm(x, w, eps=NORM_EPS):
    xf = x.astype(jnp.float32)
    y = xf * lax.rsqrt(jnp.mean(xf * xf, axis=-1, keepdims=True) + eps)
    return (y * w.astype(jnp.float32)).astype(x.dtype)


def l2_normalize(x, eps=1e-6):
    xf = x.astype(jnp.float32)
    return xf * lax.rsqrt(jnp.sum(xf * xf, axis=-1, keepdims=True) + eps)


def adaln_params(c, w_ada, b_ada):
    m = jax.nn.silu(c) @ w_ada + b_ada
    return [t[:, None, :] for t in jnp.split(m, 6, axis=-1)]


def rwkv7_recurrence(r, w, k, v, kk, a, S0):
    def step(S, inp):
        r_t, w_t, k_t, v_t, kk_t, a_t = inp
        sa = jnp.einsum('bhvk,bhk->bhv', S, -kk_t)
        S = (S * w_t[:, :, None, :] + sa[..., None] * (kk_t * a_t)[:, :, None, :]
             + v_t[..., None] * k_t[:, :, None, :])
        return S, jnp.einsum('bhvk,bhk->bhv', S, r_t)
    xs = tuple(jnp.moveaxis(t, 1, 0) for t in (r, w, k, v, kk, a))
    S, o = lax.scan(step, S0.astype(jnp.float32), xs)
    return jnp.moveaxis(o, 0, 1), S


def rwkv7_branch(pr, shift_prev, S0, lw):
    B, T, _ = pr.shape
    f32 = jnp.float32
    prev = jnp.concatenate([shift_prev[:, None, :].astype(pr.dtype), pr[:, :-1]], axis=1)
    xs = pr + (prev - pr) * lw['r_mu']
    r, k, v, dw, da, dg = jnp.split(xs, R_SPLITS, axis=-1)
    log_w = -jax.nn.softplus(-(lw['r_w0'] + jnp.tanh(dw) @ lw['r_w_w2']).astype(f32)) - 0.5
    w = jnp.exp(-jnp.exp(log_w))
    a = jax.nn.sigmoid((lw['r_a0'] + da @ lw['r_w_a2']).astype(f32))
    g = (jax.nn.sigmoid(dg) @ lw['r_w_g2']).astype(f32)
    heads = lambda t: t.astype(f32).reshape(B, T, R_HEADS, R_HEAD)
    r, k, v, w, a = heads(r), heads(k), heads(v), heads(w), heads(a)
    kk = l2_normalize(k * lw['r_k_k'].reshape(R_HEADS, R_HEAD))
    k = k * (1.0 + (a - 1.0) * lw['r_k_a'].reshape(R_HEADS, R_HEAD))
    o, S = rwkv7_recurrence(r, w, k, v, kk, a, S0)
    mu = jnp.mean(o, axis=-1, keepdims=True)
    var = jnp.mean(jnp.square(o - mu), axis=-1, keepdims=True)
    o = ((o - mu) * lax.rsqrt(var + RWKV_GN_EPS)).reshape(B, T, R_WIDTH)
    o = o * lw['r_lnx_w'] + lw['r_lnx_b']
    bonus = jnp.sum(r * k * lw['r_r_k'], axis=-1, keepdims=True) * v
    o = (o + bonus.reshape(B, T, R_WIDTH)) * g
    return o.astype(pr.dtype), S, pr[:, -1]


def gated_delta_chunked(q, k, v, log_alpha, beta, S0):
    B, T, H, Dk = q.shape
    Dv = v.shape[-1]
    C = min(G_CHUNK, T)
    n = -(-T // C)
    pad = n * C - T

    def prep(t):
        t = jnp.pad(t, [(0, 0), (0, pad)] + [(0, 0)] * (t.ndim - 2))
        t = t.reshape((B, n, C) + t.shape[2:])
        return jnp.moveaxis(jnp.moveaxis(t, 3, 2), 1, 0)

    q, k, v, la, bt = prep(q), prep(k), prep(v), prep(log_alpha), prep(beta)
    g = jnp.cumsum(la, axis=-1)
    idx = jnp.arange(C)
    causal = idx[:, None] >= idx[None, :]
    strict = idx[:, None] > idx[None, :]
    decay = jnp.exp(jnp.where(causal, g[..., :, None] - g[..., None, :], -jnp.inf))
    kb = k * bt[..., None]
    L = jnp.where(strict, jnp.einsum('nbhcd,nbhsd->nbhcs', kb, k) * decay, 0.0)
    eye = jnp.eye(C, dtype=jnp.float32)
    Tm = lax.linalg.triangular_solve(eye + L, jnp.broadcast_to(eye, L.shape), left_side=True,
                                     lower=True, unit_diagonal=True)
    u = Tm @ (v * bt[..., None])
    wk = Tm @ (kb * jnp.exp(g)[..., None])
    attn = jnp.where(causal, jnp.einsum('nbhcd,nbhsd->nbhcs', q, k) * decay, 0.0)
    q_g = q * jnp.exp(g)[..., None]
    g_last = g[..., -1:]
    k_tail = k * jnp.exp(g_last - g)[..., None]
    chunk_decay = jnp.exp(g_last)[..., None]

    def step(S, inp):
        u_i, w_i, qg_i, at_i, kt_i, cd_i = inp
        v_new = u_i - w_i @ S
        o_i = qg_i @ S + at_i @ v_new
        S = S * cd_i + jnp.swapaxes(kt_i, -1, -2) @ v_new
        return S, o_i

    S, o = lax.scan(step, S0.astype(jnp.float32), (u, wk, q_g, attn, k_tail, chunk_decay))
    o = jnp.moveaxis(jnp.moveaxis(o, 0, 1), 2, 3).reshape(B, n * C, H, Dv)[:, :T]
    return o, S


def gdn_branch(pg, conv_prev, S0, lw):
    B, T, _ = pg.shape
    f32 = jnp.float32
    qkv, z, b_raw, a_raw = jnp.split(pg, G_SPLITS, axis=-1)
    full = jnp.concatenate([conv_prev.astype(qkv.dtype), qkv], axis=1)
    cw = lw['g_conv_w']
    conv = full[:, 0:T] * cw[0]
    for i in range(1, G_CONV):
        conv = conv + full[:, i:i + T] * cw[i]
    q, k, v = jnp.split(jax.nn.silu(conv), 3, axis=-1)
    heads = lambda t: t.reshape(B, T, G_HEADS, G_HEAD)
    q = l2_normalize(heads(q)) * (G_HEAD ** -0.5)
    k = l2_normalize(heads(k))
    v = heads(v).astype(f32)
    beta = jax.nn.sigmoid(b_raw.astype(f32))
    log_alpha = -jnp.exp(lw['g_a_log'].astype(f32)) * jax.nn.softplus(a_raw.astype(f32) + lw['g_dt_bias'])
    o, S = gated_delta_chunked(q, k, v, log_alpha, beta, S0)
    o = o * lax.rsqrt(jnp.mean(o * o, axis=-1, keepdims=True) + NORM_EPS) * lw['g_norm_w']
    o = o * jax.nn.silu(heads(z).astype(f32))
    return o.reshape(B, T, G_WIDTH).astype(pg.dtype), S, full[:, -(G_CONV - 1):]


def trunk_layer(x, c, wkv0, shift0, gdn0, conv0, lw):
    sh1, sc1, gt1, sh2, sc2, gt2 = adaln_params(c, lw['w_ada'], lw['b_ada'])
    h = rms_norm(x, lw['norm1_w']) * (1.0 + sc1) + sh1
    p = h @ lw['w_in']
    pr, pg, pm = jnp.split(p, (R_COLS, R_COLS + G_COLS), axis=-1)
    ya, wkv, shift = rwkv7_branch(pr, shift0, wkv0, lw)
    yb, gdn, conv = gdn_branch(pg, conv0, gdn0, lw)
    gate_a, gate_b = jnp.split(jax.nn.sigmoid(pm), 2, axis=-1)
    merged = gate_a * (ya @ lw['w_out_a']) + gate_b * (yb @ lw['w_out_b'])
    x = x + gt1 * (merged @ lw['w_out'])
    h2 = rms_norm(x, lw['norm2_w']) * (1.0 + sc2) + sh2
    x = x + gt2 * (jnp.square(jax.nn.relu(h2 @ lw['w_up'])) @ lw['w_down'])
    return x, (wkv, shift, gdn, conv)


def run_group(x, c, wkv0, shift0, gdn0, conv0, layers, final_norm_w):
    new = []
    for l in range(DEPTH):
        x, st = trunk_layer(x, c, wkv0[l], shift0[l], gdn0[l], conv0[l], layers[l])
        new.append(st)
    y = rms_norm(x, final_norm_w)
    return y, [jnp.stack([s[i] for s in new]) for i in range(4)]


def setup_inputs(seed: int = 0) -> dict:
    key = jax.random.key(seed)
    ks = list(jax.random.split(key, 48))
    f32 = jnp.float32
    nrm = lambda shape, s=1.0: s * jax.random.normal(ks.pop(), shape, f32)
    uni = lambda shape, lo, hi: jax.random.uniform(ks.pop(), shape, f32, lo, hi)
    L = DEPTH
    inp = {}
    inp['x_prompt'] = nrm((BATCH, SEQ, D_MODEL))
    inp['x_sample'] = nrm((DEC_BATCH, DEC_SEQ, D_MODEL))
    inp['state_rwkv_wkv'] = nrm((L, DEC_BATCH, R_HEADS, R_HEAD, R_HEAD), 0.5)
    inp['state_rwkv_shift'] = nrm((L, DEC_BATCH, R_COLS))
    inp['state_gdn'] = nrm((L, DEC_BATCH, G_HEADS, G_HEAD, G_HEAD), 0.5)
    inp['state_gdn_conv'] = nrm((L, DEC_BATCH, G_CONV - 1, G_CONV_COLS))
    inp['c_prompt'] = nrm((BATCH, D_MODEL))
    inp['c_sample'] = nrm((DEC_BATCH, D_MODEL))
    inp['norm1_w'] = 1.0 + nrm((L, D_MODEL), 0.01)
    inp['norm2_w'] = 1.0 + nrm((L, D_MODEL), 0.01)
    inp['w_ada'] = nrm((L, D_MODEL, 6 * D_MODEL), D_MODEL ** -0.5)
    inp['b_ada'] = nrm((L, 6 * D_MODEL), 0.01)
    inp['w_in'] = nrm((L, D_MODEL, IN_COLS), D_MODEL ** -0.5)
    inp['r_mu'] = uni((L, R_COLS), 0.0, 1.0)
    inp['r_w0'] = uni((L, R_WIDTH), -4.0, 1.0)
    inp['r_w_w2'] = nrm((L, R_DECAY_LORA, R_WIDTH), R_DECAY_LORA ** -0.5)
    inp['r_a0'] = nrm((L, R_WIDTH), 0.1)
    inp['r_w_a2'] = nrm((L, R_AAA_LORA, R_WIDTH), R_AAA_LORA ** -0.5)
    inp['r_w_g2'] = nrm((L, R_GATE_LORA, R_WIDTH), R_GATE_LORA ** -0.5)
    inp['r_k_k'] = 0.85 + nrm((L, R_WIDTH), 0.1)
    inp['r_k_a'] = 1.0 + nrm((L, R_WIDTH), 0.1)
    inp['r_r_k'] = nrm((L, R_HEADS, R_HEAD), 0.1)
    inp['r_lnx_w'] = 1.0 + nrm((L, R_WIDTH), 0.01)
    inp['r_lnx_b'] = nrm((L, R_WIDTH), 0.01)
    inp['g_conv_w'] = nrm((L, G_CONV, G_CONV_COLS), G_CONV ** -0.5)
    inp['g_a_log'] = jnp.log(uni((L, G_HEADS), 1.0, 16.0))
    inp['g_dt_bias'] = jnp.log(jnp.expm1(uni((L, G_HEADS), 0.001, 0.1)))
    inp['g_norm_w'] = 1.0 + nrm((L, G_HEAD), 0.01)
    inp['w_out_a'] = nrm((L, R_WIDTH, D_MODEL), R_WIDTH ** -0.5)
    inp['w_out_b'] = nrm((L, G_WIDTH, D_MODEL), G_WIDTH ** -0.5)
    inp['w_out'] = nrm((L, D_MODEL, D_MODEL), D_MODEL ** -0.5)
    inp['w_up'] = nrm((L, D_MODEL, D_FF), D_MODEL ** -0.5)
    inp['w_down'] = nrm((L, D_FF, D_MODEL), D_FF ** -0.5)
    inp['final_norm_w'] = 1.0 + nrm((D_MODEL,), 0.01)
    return inp


def reference(x_prompt, x_sample, state_rwkv_wkv, state_rwkv_shift, state_gdn, state_gdn_conv,
              c_prompt, c_sample, norm1_w, norm2_w, w_ada, b_ada, w_in, r_mu, r_w0, r_w_w2, r_a0,
              r_w_a2, r_w_g2, r_k_k, r_k_a, r_r_k, r_lnx_w, r_lnx_b, g_conv_w, g_a_log, g_dt_bias,
              g_norm_w, w_out_a, w_out_b, w_out, w_up, w_down, final_norm_w):
    layers = [dict(norm1_w=norm1_w[l], norm2_w=norm2_w[l], w_ada=w_ada[l], b_ada=b_ada[l],
                   w_in=w_in[l], r_mu=r_mu[l], r_w0=r_w0[l], r_w_w2=r_w_w2[l], r_a0=r_a0[l],
                   r_w_a2=r_w_a2[l], r_w_g2=r_w_g2[l], r_k_k=r_k_k[l], r_k_a=r_k_a[l],
                   r_r_k=r_r_k[l], r_lnx_w=r_lnx_w[l], r_lnx_b=r_lnx_b[l], g_conv_w=g_conv_w[l],
                   g_a_log=g_a_log[l], g_dt_bias=g_dt_bias[l], g_norm_w=g_norm_w[l],
                   w_out_a=w_out_a[l], w_out_b=w_out_b[l], w_out=w_out[l], w_up=w_up[l],
                   w_down=w_down[l]) for l in range(DEPTH)]
    bp = x_prompt.shape[0]
    y_prompt, (p_wkv, p_shift, p_gdn, p_conv) = run_group(
        x_prompt, c_prompt,
        jnp.zeros((DEPTH, bp, R_HEADS, R_HEAD, R_HEAD), jnp.float32),
        jnp.zeros((DEPTH, bp, R_COLS), x_prompt.dtype),
        jnp.zeros((DEPTH, bp, G_HEADS, G_HEAD, G_HEAD), jnp.float32),
        jnp.zeros((DEPTH, bp, G_CONV - 1, G_CONV_COLS), x_prompt.dtype),
        layers, final_norm_w)
    y_sample, (s_wkv, s_shift, s_gdn, s_conv) = run_group(
        x_sample, c_sample, state_rwkv_wkv, state_rwkv_shift, state_gdn, state_gdn_conv,
        layers, final_norm_w)
    return (y_prompt, y_sample, p_wkv, p_shift, p_gdn, p_conv, s_wkv, s_shift, s_gdn, s_conv)
```

```python
import functools

import jax
import jax.numpy as jnp
from jax import lax
from jax.experimental import pallas as pl
from jax.experimental.pallas import tpu as pltpu

F32 = jnp.float32
BF16 = jnp.bfloat16

LANES = 128
SUBLANES = 8
VMEM_LIMIT = 48 * 1024 * 1024
VMEM_LIMIT_WIDE = 56 * 1024 * 1024

R_HEAD = 64
G_HEAD = 128
G_CONV = 4
R_HB = 8
G_HB = 4
HBW = 512
CHUNK = 64
NORM_EPS = 1e-6
RWKV_GN_EPS = 64e-5

NN = ((1,), (0,))
NT = ((1,), (1,))
TN = ((0,), (0,))


def _cparams(sem, vmem=VMEM_LIMIT):
    return pltpu.CompilerParams(dimension_semantics=sem, vmem_limit_bytes=vmem)


def _pick(n, cands):
    for c in cands:
        if n % c == 0:
            return c
    raise ValueError(f"no tile in {cands} divides {n}")


def _dot(a, b, dims=NN):
    return lax.dot_general(a, b, (dims, ((), ())), preferred_element_type=F32)


def _mm1(a, b, dims=NN):
    return _dot(a.astype(BF16), b.astype(BF16), dims)


def _split2(a):
    hi = a.astype(BF16)
    lo = (a - hi.astype(F32)).astype(BF16)
    return hi, lo


def _mm3(a, b, dims=NN):
    ah, al = _split2(a)
    bh, bl = _split2(b)
    return _dot(ah, bh, dims) + (_dot(ah, bl, dims) + _dot(al, bh, dims))


def _mm_exact_lhs(a_bf16, b, dims=NN):
    b1 = b.astype(BF16)
    r1 = b - b1.astype(F32)
    b2 = r1.astype(BF16)
    b3 = (r1 - b2.astype(F32)).astype(BF16)
    return _dot(a_bf16, b1, dims) + (_dot(a_bf16, b2, dims) + _dot(a_bf16, b3, dims))


def _sigmoid(x):
    return 1.0 / (1.0 + jnp.exp(-x))


def _silu(x):
    return x * _sigmoid(x)


def _softplus(x):
    return jnp.maximum(x, 0.0) + jnp.log1p(jnp.exp(-jnp.abs(x)))


def _tri_masks(c):
    row = lax.broadcasted_iota(jnp.int32, (c, c), 0)
    col = lax.broadcasted_iota(jnp.int32, (c, c), 1)
    return row > col, row >= col


def _inv_unit_lower(n, levels):
    c = n.shape[0]
    row = lax.broadcasted_iota(jnp.int32, (c, c), 0)
    col = lax.broadcasted_iota(jnp.int32, (c, c), 1)
    t = jnp.where(row == col, 1.0, 0.0).astype(F32) + n
    p = n
    for _ in range(levels - 1):
        p = _mm3(p, p)
        t = t + _mm3(t, p)
    return t


def _shift_rows(x, hist, k):
    c = x.shape[0]
    xr = pltpu.roll(x, k, 0)
    hr = pltpu.roll(hist, k, 0)
    row8 = lax.broadcasted_iota(jnp.int32, (SUBLANES, 1), 0)
    first = jnp.where(row8 < k, hr, xr[:SUBLANES])
    if c == SUBLANES:
        return first
    return jnp.concatenate([first, xr[SUBLANES:]], axis=0)


def _adaln_kernel(c_ref, w_ref, b_ref, o_ref):
    c = c_ref[...]
    s = _silu(c)
    o_ref[...] = _mm1(s, w_ref[...]) + b_ref[...]


def _adaln(c_all, w_ada, b_ada):
    m, d = c_all.shape
    n = w_ada.shape[1]
    tn = _pick(n, (512, 256, 128))
    return pl.pallas_call(
        _adaln_kernel,
        grid=(n // tn,),
        in_specs=[pl.BlockSpec((m, d), lambda j: (0, 0)),
                  pl.BlockSpec((d, tn), lambda j: (0, j)),
                  pl.BlockSpec((1, tn), lambda j: (0, j))],
        out_specs=pl.BlockSpec((m, tn), lambda j: (0, j)),
        out_shape=jax.ShapeDtypeStruct((m, n), F32),
        compiler_params=_cparams(("parallel",)),
        name="adaln",
    )(c_all, w_ada, b_ada.reshape(1, n))


class _Mods:
    def __init__(self, arr, per_seq, d, rows_per_seq):
        self.arr, self.per_seq, self.d, self.rows_per_seq = arr, per_seq, d, rows_per_seq

    def spec(self, tm, tn, which, ij):
        off = which * self.d // tn
        if self.per_seq:
            tps = self.rows_per_seq // tm
            return pl.BlockSpec((1, 1, tn), lambda *g: (ij(*g)[0] // tps, 0, off + ij(*g)[1]))
        return pl.BlockSpec((tm, tn), lambda *g: (ij(*g)[0], off + ij(*g)[1]))


def _mod_val(ref):
    v = ref[...]
    return v[0] if v.ndim == 3 else v


def _norm_mod_kernel(x_ref, w_ref, sc_ref, sh_ref, o_ref):
    x = x_ref[...]
    y = x * lax.rsqrt(jnp.mean(x * x, axis=-1, keepdims=True) + NORM_EPS) * w_ref[...]
    o_ref[...] = (y * (1.0 + _mod_val(sc_ref)) + _mod_val(sh_ref)).astype(o_ref.dtype)


def _norm_mod(x, w, mods, sh_idx, sc_idx, tm):
    m, d = x.shape
    ij = lambda i: (i, 0)
    return pl.pallas_call(
        _norm_mod_kernel,
        grid=(m // tm,),
        in_specs=[pl.BlockSpec((tm, d), lambda i: (i, 0)),
                  pl.BlockSpec((1, d), lambda i: (0, 0)),
                  mods.spec(tm, d, sc_idx, ij),
                  mods.spec(tm, d, sh_idx, ij)],
        out_specs=pl.BlockSpec((tm, d), lambda i: (i, 0)),
        out_shape=jax.ShapeDtypeStruct((m, d), BF16),
        compiler_params=_cparams(("parallel",)),
        name="norm_mod",
    )(x, w.reshape(1, d), mods.arr, mods.arr)


def _mm_kernel(a_ref, w_ref, o_ref):
    o_ref[...] = _dot(a_ref[...], w_ref[...]).astype(o_ref.dtype)


def _matmul(a, w, tm, tn, out_dtype, name):
    m, k = a.shape
    n = w.shape[1]
    return pl.pallas_call(
        _mm_kernel,
        grid=(n // tn, m // tm),
        in_specs=[pl.BlockSpec((tm, k), lambda j, i: (i, 0)),
                  pl.BlockSpec((k, tn), lambda j, i: (0, j))],
        out_specs=pl.BlockSpec((tm, tn), lambda j, i: (i, j)),
        out_shape=jax.ShapeDtypeStruct((m, n), out_dtype),
        compiler_params=_cparams(("parallel", "parallel")),
        name=name,
    )(a, w)


def _relu2_kernel(a_ref, w_ref, o_ref):
    u = jnp.maximum(_dot(a_ref[...], w_ref[...]), 0.0)
    o_ref[...] = (u * u).astype(o_ref.dtype)


def _up_proj(a, w, tm, tn):
    m, k = a.shape
    n = w.shape[1]
    return pl.pallas_call(
        _relu2_kernel,
        grid=(n // tn, m // tm),
        in_specs=[pl.BlockSpec((tm, k), lambda j, i: (i, 0)),
                  pl.BlockSpec((k, tn), lambda j, i: (0, j))],
        out_specs=pl.BlockSpec((tm, tn), lambda j, i: (i, j)),
        out_shape=jax.ShapeDtypeStruct((m, n), BF16),
        compiler_params=_cparams(("parallel", "parallel")),
        name="up_proj",
    )(a, w)


def _merge_kernel(ya_ref, yb_ref, wa_ref, wb_ref, ga_ref, gb_ref, o_ref):
    pa = _dot(ya_ref[...], wa_ref[...])
    pb = _dot(yb_ref[...], wb_ref[...])
    o_ref[...] = (_sigmoid(ga_ref[...]) * pa + _sigmoid(gb_ref[...]) * pb).astype(o_ref.dtype)


def _merge(ya, yb, wa, wb, p_all, moff, tm, tn):
    m, ka = ya.shape
    kb = yb.shape[1]
    d = wa.shape[1]
    oa, ob = moff // tn, (moff + d) // tn
    return pl.pallas_call(
        _merge_kernel,
        grid=(d // tn, m // tm),
        in_specs=[pl.BlockSpec((tm, ka), lambda j, i: (i, 0)),
                  pl.BlockSpec((tm, kb), lambda j, i: (i, 0)),
                  pl.BlockSpec((ka, tn), lambda j, i: (0, j)),
                  pl.BlockSpec((kb, tn), lambda j, i: (0, j)),
                  pl.BlockSpec((tm, tn), lambda j, i: (i, oa + j)),
                  pl.BlockSpec((tm, tn), lambda j, i: (i, ob + j))],
        out_specs=pl.BlockSpec((tm, tn), lambda j, i: (i, j)),
        out_shape=jax.ShapeDtypeStruct((m, d), BF16),
        compiler_params=_cparams(("parallel", "parallel")),
        name="merge",
    )(ya, yb, wa, wb, p_all, p_all)


def _resid_kernel(a_ref, w_ref, x_ref, g_ref, o_ref):
    o_ref[...] = x_ref[...] + _mod_val(g_ref) * _dot(a_ref[...], w_ref[...])


def _out_resid(a, w, x, mods, gate_idx, tm, tn):
    m, k = a.shape
    n = w.shape[1]
    ij = lambda j, i: (i, j)
    return pl.pallas_call(
        _resid_kernel,
        grid=(n // tn, m // tm),
        in_specs=[pl.BlockSpec((tm, k), lambda j, i: (i, 0)),
                  pl.BlockSpec((k, tn), lambda j, i: (0, j)),
                  pl.BlockSpec((tm, tn), lambda j, i: (i, j)),
                  mods.spec(tm, tn, gate_idx, ij)],
        out_specs=pl.BlockSpec((tm, tn), lambda j, i: (i, j)),
        out_shape=jax.ShapeDtypeStruct((m, n), F32),
        compiler_params=_cparams(("parallel", "parallel")),
        name="out_resid",
    )(a, w, x, mods.arr)


def _down_kernel(a_ref, w_ref, x_ref, g_ref, fw_ref, o_ref):
    kk = pl.program_id(1)
    part = _dot(a_ref[...], w_ref[...])

    @pl.when(kk == 0)
    def _():
        o_ref[...] = part

    @pl.when(kk > 0)
    def _():
        o_ref[...] += part

    @pl.when(kk == pl.num_programs(1) - 1)
    def _():
        x2 = x_ref[...] + _mod_val(g_ref) * o_ref[...]
        y = x2 * lax.rsqrt(jnp.mean(x2 * x2, axis=-1, keepdims=True) + NORM_EPS)
        o_ref[...] = y * fw_ref[...]


def _down_final(a, w, x, mods, gate_idx, final_w, tm, tk):
    m, k = a.shape
    n = w.shape[1]
    ij = lambda i, kk: (i, 0)
    return pl.pallas_call(
        _down_kernel,
        grid=(m // tm, k // tk),
        in_specs=[pl.BlockSpec((tm, tk), lambda i, kk: (i, kk)),
                  pl.BlockSpec((tk, n), lambda i, kk: (kk, 0)),
                  pl.BlockSpec((tm, n), lambda i, kk: (i, 0), pipeline_mode=pl.Buffered(1)),
                  mods.spec(tm, n, gate_idx, ij),
                  pl.BlockSpec((1, n), lambda i, kk: (0, 0))],
        out_specs=pl.BlockSpec((tm, n), lambda i, kk: (i, 0)),
        out_shape=jax.ShapeDtypeStruct((m, n), F32),
        compiler_params=_cparams(("parallel", "arbitrary"), VMEM_LIMIT_WIDE),
        name="down_final",
    )(a, w, x, mods.arr, final_w.reshape(1, n))


def _rwkv_kernel(*refs, c, t_valid, has_state, levels):
    (r_ref, k_ref, v_ref, lo_ref, mur_ref, muk_ref, muv_ref, mul_ref,
     w0_ref, a0_ref, kk_ref, ka_ref, rk_ref, lw_ref, lb_ref,
     ww2_ref, wa2_ref, wg2_ref) = refs[:18]
    n = 18
    if has_state:
        s0_ref, shr_ref, shk_ref, shv_ref, shl_ref = refs[n:n + 5]
        n += 5
    ya_ref, sout_ref, s_ref, prev_ref = refs[n:n + 4]
    ci = pl.program_id(2)

    @pl.when(ci == 0)
    def _():
        if has_state:
            s_ref[...] = s0_ref[0]
            prev_ref[0, 0:1, :] = shr_ref[0]
            prev_ref[1, 0:1, :] = shk_ref[0]
            prev_ref[2, 0:1, :] = shv_ref[0]
            prev_ref[3, 0:1, :] = shl_ref[0]
        else:
            s_ref[...] = jnp.zeros_like(s_ref)
            prev_ref[...] = jnp.zeros_like(prev_ref)

    row = lax.broadcasted_iota(jnp.int32, (c, 1), 0)
    is_row0 = row == 0

    def tshift(x_ref, slot, mu_ref):
        x = x_ref[...]
        prev = jnp.where(is_row0, prev_ref[slot, 0:1, :], pltpu.roll(x, 1, 0))
        prev_ref[slot, 0:1, :] = x[c - 1:c, :]
        return x + (prev - x) * mu_ref[...]

    r = tshift(r_ref, 0, mur_ref)
    k = tshift(k_ref, 1, muk_ref)
    v = tshift(v_ref, 2, muv_ref)
    lo = tshift(lo_ref, 3, mul_ref)
    dlw = ww2_ref.shape[0]
    dla = wa2_ref.shape[0]
    dlg = wg2_ref.shape[0]
    dw = lo[:, 0:dlw]
    da = lo[:, dlw:dlw + dla]
    dg = lo[:, dlw + dla:dlw + dla + dlg]

    log_w = -_softplus(-(w0_ref[...] + _mm3(jnp.tanh(dw), ww2_ref[...]))) - 0.5
    lw = -jnp.exp(log_w)
    a = _sigmoid(a0_ref[...] + _mm3(da, wa2_ref[...]))
    g = _mm3(_sigmoid(dg), wg2_ref[...])

    if t_valid is not None:
        valid = (row < t_valid).astype(F32)
        lw = lw * valid
        k = k * valid
        v = v * valid

    strict, causal = _tri_masks(c)
    ltri = jnp.where(causal, 1.0, 0.0).astype(BF16)
    cum = _mm_exact_lhs(ltri, lw)

    kkw = kk_ref[...]
    kaw = ka_ref[...]
    rkw = rk_ref[...]
    lnw = lw_ref[...]
    lnb = lb_ref[...]
    nh = r.shape[1] // R_HEAD
    outs = []
    for h in range(nh):
        sl = slice(h * R_HEAD, (h + 1) * R_HEAD)
        r_h, k_h, v_h, a_h = r[:, sl], k[:, sl], v[:, sl], a[:, sl]
        lw_h, cum_h = lw[:, sl], cum[:, sl]
        kx = k_h * kkw[:, sl]
        kk_h = kx * lax.rsqrt(jnp.sum(kx * kx, axis=-1, keepdims=True) + 1e-6)
        k2_h = k_h * (1.0 + (a_h - 1.0) * kaw[:, sl])
        w_inc = jnp.exp(cum_h)
        w_inv = jnp.exp(-cum_h)
        w_exc = jnp.exp(cum_h - lw_h)
        rp = r_h * w_inc
        ap = -kk_h * w_exc
        bp = kk_h * a_h * w_inv
        kp = k2_h * w_inv
        x = jnp.concatenate([ap, rp], axis=0)
        y = jnp.concatenate([bp, kp], axis=0)
        s_h = s_ref[h]
        amat = _mm3(x, y, NT)
        xs = _mm3(x, s_h, NT)
        a_ab = jnp.where(strict, amat[:c, :c], 0.0)
        a_ak = jnp.where(strict, amat[:c, c:], 0.0)
        a_rb = jnp.where(causal, amat[c:, :c], 0.0)
        a_rk = jnp.where(causal, amat[c:, c:], 0.0)
        tinv = _inv_unit_lower(a_ab, levels)
        u = _mm3(tinv, xs[:c] + _mm3(a_ak, v_h))
        o = xs[c:] + _mm3(a_rb, u) + _mm3(a_rk, v_h)
        s_new = (s_h + _mm3(u, bp, TN) + _mm3(v_h, kp, TN)) * w_inc[c - 1:c, :]
        s_ref[h] = s_new
        mu = jnp.mean(o, axis=-1, keepdims=True)
        var = jnp.mean(jnp.square(o - mu), axis=-1, keepdims=True)
        on = (o - mu) * lax.rsqrt(var + RWKV_GN_EPS) * lnw[:, sl] + lnb[:, sl]
        bonus = jnp.sum(r_h * k2_h * rkw[:, sl], axis=-1, keepdims=True) * v_h
        outs.append((on + bonus) * g[:, sl])
    ya_ref[...] = jnp.concatenate(outs, axis=1).astype(ya_ref.dtype)

    @pl.when(ci == pl.num_programs(2) - 1)
    def _():
        sout_ref[0] = s_ref[...]


def _rwkv(p_all, nb, t_pad, t_valid, rw, prm, state):
    c = min(CHUNK, t_pad)
    nc = t_pad // c
    nhb = rw // HBW
    heads = rw // R_HEAD
    kb, vb, lb = rw // HBW, 2 * rw // HBW, 3 * rw // HBW
    has_state = state is not None
    levels = max(1, (min(c, t_valid or c) - 1).bit_length())

    def pcol(off):
        return pl.BlockSpec((c, HBW), lambda b, h, ci: (b * nc + ci, off + h))

    def vec(off):
        return pl.BlockSpec((1, HBW), lambda b, h, ci: (0, off + h))

    in_specs = [pcol(0), pcol(kb), pcol(vb),
                pl.BlockSpec((c, HBW), lambda b, h, ci: (b * nc + ci, lb)),
                vec(0), vec(kb), vec(vb),
                pl.BlockSpec((1, HBW), lambda b, h, ci: (0, lb)),
                vec(0), vec(0), vec(0), vec(0), vec(0), vec(0), vec(0),
                pl.BlockSpec((prm["ww2"].shape[0], HBW), lambda b, h, ci: (0, h)),
                pl.BlockSpec((prm["wa2"].shape[0], HBW), lambda b, h, ci: (0, h)),
                pl.BlockSpec((prm["wg2"].shape[0], HBW), lambda b, h, ci: (0, h))]
    args = [p_all, p_all, p_all, p_all, prm["mu"], prm["mu"], prm["mu"], prm["mu"],
            prm["w0"], prm["a0"], prm["k_k"], prm["k_a"], prm["r_k"], prm["lnx_w"], prm["lnx_b"],
            prm["ww2"], prm["wa2"], prm["wg2"]]
    if has_state:
        s0, shift = state
        in_specs += [pl.BlockSpec((1, R_HB, R_HEAD, R_HEAD), lambda b, h, ci: (b, h, 0, 0)),
                     pl.BlockSpec((1, 1, HBW), lambda b, h, ci: (b, 0, h)),
                     pl.BlockSpec((1, 1, HBW), lambda b, h, ci: (b, 0, kb + h)),
                     pl.BlockSpec((1, 1, HBW), lambda b, h, ci: (b, 0, vb + h)),
                     pl.BlockSpec((1, 1, HBW), lambda b, h, ci: (b, 0, lb))]
        args += [s0, shift, shift, shift, shift]
    kern = functools.partial(_rwkv_kernel, c=c, t_valid=t_valid, has_state=has_state, levels=levels)
    return pl.pallas_call(
        kern,
        grid=(nb, nhb, nc),
        in_specs=in_specs,
        out_specs=[pl.BlockSpec((c, HBW), lambda b, h, ci: (b * nc + ci, h)),
                   pl.BlockSpec((1, R_HB, R_HEAD, R_HEAD), lambda b, h, ci: (b, h, 0, 0))],
        out_shape=[jax.ShapeDtypeStruct((nb * t_pad, rw), BF16),
                   jax.ShapeDtypeStruct((nb, heads, R_HEAD, R_HEAD), F32)],
        scratch_shapes=[pltpu.VMEM((R_HB, R_HEAD, R_HEAD), F32),
                        pltpu.VMEM((4, SUBLANES, HBW), F32)],
        compiler_params=_cparams(("parallel", "parallel", "arbitrary")),
        name="rwkv7_chunk",
    )(*args)


def _gdn_kernel(*refs, c, t_valid, has_state, levels):
    (q_ref, k_ref, v_ref, z_ref, ba_ref, cq_ref, ck_ref, cv_ref,
     an_ref, dt_ref, nw_ref) = refs[:11]
    n = 11
    if has_state:
        s0_ref, hq_ref, hk_ref, hv_ref = refs[n:n + 4]
        n += 4
    yb_ref, sout_ref, s_ref, hist_ref = refs[n:n + 4]
    ci = pl.program_id(2)

    @pl.when(ci == 0)
    def _():
        if has_state:
            s_ref[...] = s0_ref[0]
            hist_ref[0] = hq_ref[0]
            hist_ref[1] = hk_ref[0]
            hist_ref[2] = hv_ref[0]
        else:
            s_ref[...] = jnp.zeros_like(s_ref)
            hist_ref[...] = jnp.zeros_like(hist_ref)

    row = lax.broadcasted_iota(jnp.int32, (c, 1), 0)

    def conv(x_ref, slot, cw_ref):
        x = x_ref[...]
        hist = hist_ref[slot]
        cw = cw_ref[...]
        acc = _shift_rows(x, hist, 3) * cw[0:1, :]
        acc = acc + _shift_rows(x, hist, 2) * cw[1:2, :]
        acc = acc + _shift_rows(x, hist, 1) * cw[2:3, :]
        acc = acc + x * cw[3:4, :]
        hist_ref[slot] = x[c - SUBLANES:c, :]
        return _silu(acc)

    q = conv(q_ref, 0, cq_ref)
    k = conv(k_ref, 1, ck_ref)
    v = conv(v_ref, 2, cv_ref)
    z = z_ref[...]
    ba = ba_ref[...]
    beta_all = _sigmoid(ba)
    la_all = an_ref[...] * _softplus(ba + dt_ref[...])
    if t_valid is not None:
        valid = (row < t_valid).astype(F32)
        beta_all = beta_all * valid
        la_all = la_all * valid
        q = q * valid
        k = k * valid
        v = v * valid

    strict, causal = _tri_masks(c)
    ltri = jnp.where(causal, 1.0, 0.0).astype(BF16)
    g_all = _mm_exact_lhs(ltri, la_all)
    lane = lax.broadcasted_iota(jnp.int32, (1, LANES), 1)
    ones = jnp.ones((c, LANES), BF16)
    nw = nw_ref[...]
    nh = q.shape[1] // G_HEAD
    outs = []
    for h in range(nh):
        sl = slice(h * G_HEAD, (h + 1) * G_HEAD)
        gcol = g_all[:, G_HB + h:G_HB + h + 1]
        beta = beta_all[:, h:h + 1]
        grow = _mm_exact_lhs(ones, jnp.where(lane == G_HB + h, g_all, 0.0), NT)
        diff = jnp.where(causal, gcol - grow, 0.0)
        decay = jnp.where(causal, jnp.exp(diff), 0.0)
        q_h, k_h, v_h = q[:, sl], k[:, sl], v[:, sl]
        q_h = q_h * lax.rsqrt(jnp.sum(q_h * q_h, axis=-1, keepdims=True) + 1e-6) * (G_HEAD ** -0.5)
        k_h = k_h * lax.rsqrt(jnp.sum(k_h * k_h, axis=-1, keepdims=True) + 1e-6)
        eg = jnp.exp(gcol)
        kb = k_h * beta
        amat = _mm3(jnp.concatenate([kb, q_h], axis=0), k_h, NT)
        lmat = jnp.where(strict, amat[:c] * decay, 0.0)
        attn = amat[c:] * decay
        tinv = _inv_unit_lower(-lmat, levels)
        uw = _mm3(tinv, jnp.concatenate([v_h * beta, kb * eg], axis=1))
        u, wk = uw[:, :G_HEAD], uw[:, G_HEAD:]
        s_h = s_ref[h]
        xs = _mm3(jnp.concatenate([wk, q_h * eg], axis=0), s_h)
        v_new = u - xs[:c]
        o = xs[c:] + _mm3(attn, v_new)
        g_last = gcol[c - 1:c, :]
        k_tail = k_h * jnp.exp(g_last - gcol)
        s_ref[h] = s_h * jnp.exp(g_last) + _mm3(k_tail, v_new, TN)
        o = o * lax.rsqrt(jnp.mean(o * o, axis=-1, keepdims=True) + NORM_EPS) * nw
        outs.append(o * _silu(z[:, sl]))
    yb_ref[...] = jnp.concatenate(outs, axis=1).astype(yb_ref.dtype)

    @pl.when(ci == pl.num_programs(2) - 1)
    def _():
        sout_ref[0] = s_ref[...]


def _gdn(p_all, nb, t_pad, t_valid, gw, goff, baoff, prm, state):
    c = min(CHUNK, t_pad)
    nc = t_pad // c
    nhb = gw // HBW
    heads = gw // G_HEAD
    qb = goff // HBW
    kb, vb, zb = qb + gw // HBW, qb + 2 * gw // HBW, qb + 3 * gw // HBW
    bab = baoff // LANES
    has_state = state is not None
    levels = max(1, (min(c, t_valid or c) - 1).bit_length())

    def pcol(off):
        return pl.BlockSpec((c, HBW), lambda b, h, ci: (b * nc + ci, off + h))

    def cwspec(off):
        return pl.BlockSpec((G_CONV, HBW), lambda b, h, ci: (0, off + h))

    in_specs = [pcol(qb), pcol(kb), pcol(vb), pcol(zb),
                pl.BlockSpec((c, LANES), lambda b, h, ci: (b * nc + ci, bab + h)),
                cwspec(0), cwspec(gw // HBW), cwspec(2 * gw // HBW),
                pl.BlockSpec((1, LANES), lambda b, h, ci: (0, h)),
                pl.BlockSpec((1, LANES), lambda b, h, ci: (0, h)),
                pl.BlockSpec((1, G_HEAD), lambda b, h, ci: (0, 0))]
    args = [p_all, p_all, p_all, p_all, p_all, prm["conv_w"], prm["conv_w"], prm["conv_w"],
            prm["neg_a"], prm["dt"], prm["norm_w"]]
    if has_state:
        s0, hist = state
        in_specs += [pl.BlockSpec((1, G_HB, G_HEAD, G_HEAD), lambda b, h, ci: (b, h, 0, 0)),
                     pl.BlockSpec((1, SUBLANES, HBW), lambda b, h, ci: (b, 0, h)),
                     pl.BlockSpec((1, SUBLANES, HBW), lambda b, h, ci: (b, 0, gw // HBW + h)),
                     pl.BlockSpec((1, SUBLANES, HBW), lambda b, h, ci: (b, 0, 2 * gw // HBW + h))]
        args += [s0, hist, hist, hist]
    kern = functools.partial(_gdn_kernel, c=c, t_valid=t_valid, has_state=has_state, levels=levels)
    return pl.pallas_call(
        kern,
        grid=(nb, nhb, nc),
        in_specs=in_specs,
        out_specs=[pl.BlockSpec((c, HBW), lambda b, h, ci: (b * nc + ci, h)),
                   pl.BlockSpec((1, G_HB, G_HEAD, G_HEAD), lambda b, h, ci: (b, h, 0, 0))],
        out_shape=[jax.ShapeDtypeStruct((nb * t_pad, gw), BF16),
                   jax.ShapeDtypeStruct((nb, heads, G_HEAD, G_HEAD), F32)],
        scratch_shapes=[pltpu.VMEM((G_HB, G_HEAD, G_HEAD), F32),
                        pltpu.VMEM((3, SUBLANES, HBW), F32)],
        compiler_params=_cparams(("parallel", "parallel", "arbitrary")),
        name="gdn_chunk",
    )(*args)


def _run_group(x, mods, nb, t, lay, wts, rprm, gprm, states):
    m, d = x.shape
    rw, gw = lay["rw"], lay["gw"]
    tm = _pick(m, (512, 256, 128, 64, 32, 16, 8))
    if mods.per_seq:
        tm = _pick(t, (512, 256, 128, 64, 32, 16, 8))
    n_all = wts["w_all"].shape[1]
    tm_norm = min(tm, 256)

    h = _norm_mod(x, wts["norm1_w"], mods, 0, 1, tm_norm)
    p_all = _matmul(h, wts["w_all"], tm, _pick(n_all, (1024, 512, 256, 128)), F32, "in_proj")

    p3 = p_all.reshape(nb, t, n_all)
    shift_new = p3[:, t - 1, :lay["r_cols"]]
    conv_new = p3[:, t - (G_CONV - 1):, lay["goff"]:lay["goff"] + 3 * gw]

    if t % SUBLANES == 0:
        t_pad, t_valid, p_rec = t, None, p_all
    else:
        t_pad = -(-t // SUBLANES) * SUBLANES
        t_valid = t
        p_rec = jnp.pad(p3, ((0, 0), (0, t_pad - t), (0, 0))).reshape(nb * t_pad, n_all)

    ya, wkv_new = _rwkv(p_rec, nb, t_pad, t_valid, rw, rprm, states and states["rwkv"])
    yb, gdn_new = _gdn(p_rec, nb, t_pad, t_valid, gw, lay["goff"], lay["baoff"], gprm,
                       states and states["gdn"])
    if t_pad != t:
        ya = ya.reshape(nb, t_pad, rw)[:, :t].reshape(m, rw)
        yb = yb.reshape(nb, t_pad, gw)[:, :t].reshape(m, gw)

    tn_m = _pick(d, (1024, 512, 256, 128))
    while lay["moff"] % tn_m:
        tn_m //= 2
    merged = _merge(ya, yb, wts["w_out_a"], wts["w_out_b"], p_all, lay["moff"], tm, tn_m)
    x1 = _out_resid(merged, wts["w_out"], x, mods, 2, tm, _pick(d, (1024, 512, 256, 128)))
    h2 = _norm_mod(x1, wts["norm2_w"], mods, 3, 4, tm_norm)
    dff = wts["w_up"].shape[1]
    up = _up_proj(h2, wts["w_up"], tm, _pick(dff, (1024, 512, 256, 128)))
    y = _down_final(up, wts["w_down"], x1, mods, 5, wts["final_norm_w"], tm,
                    _pick(dff, (512, 256, 128)))
    return y, wkv_new, shift_new, gdn_new, conv_new


def kernel(x_prompt, x_sample, state_rwkv_wkv, state_rwkv_shift, state_gdn, state_gdn_conv,
           c_prompt, c_sample, norm1_w, norm2_w, w_ada, b_ada, w_in, r_mu, r_w0, r_w_w2, r_a0,
           r_w_a2, r_w_g2, r_k_k, r_k_a, r_r_k, r_lnx_w, r_lnx_b, g_conv_w, g_a_log, g_dt_bias,
           g_norm_w, w_out_a, w_out_b, w_out, w_up, w_down, final_norm_w):
    depth = w_in.shape[0]
    assert depth == 1, "single-layer trunk"
    bp, tp, d = x_prompt.shape
    bs, ts, _ = x_sample.shape
    rw = r_w0.shape[1]
    gw = g_conv_w.shape[2] // 3
    g_heads = g_a_log.shape[1]
    dlw, dla, dlg = r_w_w2.shape[1], r_w_a2.shape[1], r_w_g2.shape[1]
    lora = dlw + dla + dlg
    r_cols = 3 * rw + lora
    g_cols = 4 * gw + 2 * g_heads
    assert rw % HBW == 0 and gw % HBW == 0 and lora <= HBW and g_heads % G_HB == 0
    assert w_in.shape[2] == r_cols + g_cols + 2 * d

    rseg = 3 * rw + HBW
    goff = rseg
    baoff = goff + 4 * gw
    n_ba = (g_heads // G_HB) * LANES
    moff = baoff + n_ba
    lay = dict(rw=rw, gw=gw, r_cols=r_cols, goff=goff, baoff=baoff, moff=moff)

    wi = w_in[0]
    w_b = wi[:, r_cols + 4 * gw:r_cols + 4 * gw + g_heads].reshape(d, g_heads // G_HB, G_HB)
    w_a = wi[:, r_cols + 4 * gw + g_heads:r_cols + g_cols].reshape(d, g_heads // G_HB, G_HB)
    w_ba = jnp.concatenate([w_b, w_a, jnp.zeros((d, g_heads // G_HB, LANES - 2 * G_HB), F32)],
                           axis=2).reshape(d, n_ba)
    w_all = jnp.concatenate([wi[:, :r_cols], jnp.zeros((d, rseg - r_cols), F32),
                             wi[:, r_cols:r_cols + 4 * gw], w_ba,
                             wi[:, r_cols + g_cols:]], axis=1).astype(BF16)

    wts = dict(w_all=w_all, norm1_w=norm1_w[0], norm2_w=norm2_w[0],
               w_out_a=w_out_a[0].astype(BF16), w_out_b=w_out_b[0].astype(BF16),
               w_out=w_out[0].astype(BF16), w_up=w_up[0].astype(BF16),
               w_down=w_down[0].astype(BF16), final_norm_w=final_norm_w)

    row = lambda a: a.reshape(1, -1)
    rprm = dict(mu=jnp.pad(row(r_mu[0]), ((0, 0), (0, rseg - r_cols))),
                w0=row(r_w0[0]), a0=row(r_a0[0]), k_k=row(r_k_k[0]), k_a=row(r_k_a[0]),
                r_k=row(r_r_k[0]), lnx_w=row(r_lnx_w[0]), lnx_b=row(r_lnx_b[0]),
                ww2=r_w_w2[0], wa2=r_w_a2[0], wg2=r_w_g2[0])

    def ba_row(vals):
        v = vals.reshape(g_heads // G_HB, G_HB)
        z = jnp.zeros_like(v)
        return jnp.concatenate([z, v, jnp.zeros((g_heads // G_HB, LANES - 2 * G_HB), F32)],
                               axis=1).reshape(1, n_ba)

    gprm = dict(conv_w=g_conv_w[0], neg_a=ba_row(-jnp.exp(g_a_log[0])), dt=ba_row(g_dt_bias[0]),
                norm_w=row(g_norm_w[0]))

    c_all = jnp.concatenate([c_prompt, c_sample], axis=0)
    n_seq = bp + bs
    c_all = jnp.pad(c_all, ((0, -n_seq % SUBLANES), (0, 0)))
    mods_all = _adaln(c_all, w_ada[0], b_ada[0])
    mods_p = _Mods(mods_all[:bp].reshape(bp, 1, 6 * d), True, d, tp)
    mods_s = _Mods(jnp.repeat(mods_all[bp:n_seq], ts, axis=0), False, d, ts)

    y_p, p_wkv, p_shift, p_gdn, p_conv = _run_group(
        x_prompt.reshape(bp * tp, d), mods_p, bp, tp, lay, wts, rprm, gprm, None)

    shift0 = jnp.pad(state_rwkv_shift[0], ((0, 0), (0, rseg - r_cols))).reshape(bs, 1, rseg)
    hist0 = jnp.pad(state_gdn_conv[0], ((0, 0), (SUBLANES - (G_CONV - 1), 0), (0, 0)))
    states = dict(rwkv=(state_rwkv_wkv[0], shift0), gdn=(state_gdn[0], hist0))
    y_s, s_wkv, s_shift, s_gdn, s_conv = _run_group(
        x_sample.reshape(bs * ts, d), mods_s, bs, ts, lay, wts, rprm, gprm, states)

    return (y_p.reshape(bp, tp, d), y_s.reshape(bs, ts, d),
            p_wkv[None], p_shift[None], p_gdn[None], p_conv[None],
            s_wkv[None], s_shift[None], s_gdn[None], s_conv[None])
```

```python
import functools

import jax
import jax.numpy as jnp
from jax import lax
from jax.experimental import pallas as pl
from jax.experimental.pallas import tpu as pltpu

F32 = jnp.float32
BF16 = jnp.bfloat16

LANES = 128
SUBLANES = 8
VMEM_LIMIT = 48 * 1024 * 1024
VMEM_LIMIT_WIDE = 56 * 1024 * 1024

R_HEAD = 64
G_HEAD = 128
G_CONV = 4
R_HB = 8
G_HB = 4
HBW = 512
CHUNK = 64
SEQ_BLOCK_ROWS = 64
NORM_EPS = 1e-6
RWKV_GN_EPS = 64e-5
PASSES_MAIN = 1
R_PASSES_INV = 3
G_PASSES_INV = 3

NN = ((1,), (0,))
NT = ((1,), (1,))
TN = ((0,), (0,))


def _cparams(sem, vmem=VMEM_LIMIT):
    return pltpu.CompilerParams(dimension_semantics=sem, vmem_limit_bytes=vmem)


def _pick(n, cands):
    for c in cands:
        if n % c == 0:
            return c
    raise ValueError(f"no tile in {cands} divides {n}")


def _dot(a, b, dims=NN):
    return lax.dot_general(a, b, (dims, ((), ())), preferred_element_type=F32)


def _mm1(a, b, dims=NN):
    return _dot(a.astype(BF16), b.astype(BF16), dims)


def _split2(a):
    hi = a.astype(BF16)
    lo = (a - hi.astype(F32)).astype(BF16)
    return hi, lo


def _mm3(a, b, dims=NN):
    ah, al = _split2(a)
    bh, bl = _split2(b)
    return _dot(ah, bh, dims) + (_dot(ah, bl, dims) + _dot(al, bh, dims))


def _mm(a, b, dims=NN, passes=1):
    return _mm1(a, b, dims) if passes == 1 else _mm3(a, b, dims)


def _mm_exact_lhs(a_bf16, b, dims=NN):
    b1 = b.astype(BF16)
    r1 = b - b1.astype(F32)
    b2 = r1.astype(BF16)
    b3 = (r1 - b2.astype(F32)).astype(BF16)
    return _dot(a_bf16, b1, dims) + (_dot(a_bf16, b2, dims) + _dot(a_bf16, b3, dims))


def _sigmoid(x):
    return 1.0 / (1.0 + jnp.exp(-x))


def _silu(x):
    return x * _sigmoid(x)


def _softplus(x):
    return jnp.maximum(x, 0.0) + jnp.log1p(jnp.exp(-jnp.abs(x)))


def _tri_masks(c, reps=1):
    row = lax.broadcasted_iota(jnp.int32, (c, reps * c), 0)
    col = lax.broadcasted_iota(jnp.int32, (c, reps * c), 1) % c
    return row > col, row >= col


def _block_cumsum_matrix(rows, tb):
    row = lax.broadcasted_iota(jnp.int32, (rows, rows), 0)
    col = lax.broadcasted_iota(jnp.int32, (rows, rows), 1)
    keep = (row >= col) & (row // tb == col // tb)
    return jnp.where(keep, 1.0, 0.0).astype(BF16)


def _inv_unit_lower(ns, levels, passes):
    c = ns[0].shape[0]
    row = lax.broadcasted_iota(jnp.int32, (c, c), 0)
    col = lax.broadcasted_iota(jnp.int32, (c, c), 1)
    eye = jnp.where(row == col, 1.0, 0.0).astype(F32)
    ts = [eye + n for n in ns]
    ps = ns
    for _ in range(levels - 1):
        ps = [_mm(p, p, NN, passes) for p in ps]
        ts = [t + _mm(t, p, NN, passes) for t, p in zip(ts, ps)]
    return ts


def _shift_rows(x, hists, k, tb):
    rows = x.shape[0]
    xr = pltpu.roll(x, k, 0)
    hr = [pltpu.roll(h, k, 0) for h in hists]
    if tb == SUBLANES:
        hfull = hr[0] if len(hr) == 1 else jnp.concatenate(hr, axis=0)
        row = lax.broadcasted_iota(jnp.int32, (rows, 1), 0)
        return jnp.where(row % tb < k, hfull, xr)
    assert len(hists) == 1 and rows == tb
    row8 = lax.broadcasted_iota(jnp.int32, (SUBLANES, 1), 0)
    first = jnp.where(row8 < k, hr[0], xr[:SUBLANES])
    return jnp.concatenate([first, xr[SUBLANES:]], axis=0)


def _rows_from(parts, tb):
    tiles = [jnp.broadcast_to(p, (tb, p.shape[1])) for p in parts]
    return tiles[0] if len(tiles) == 1 else jnp.concatenate(tiles, axis=0)


def _adaln_kernel(c_ref, w_ref, b_ref, o_ref):
    c = c_ref[...]
    s = _silu(c)
    o_ref[...] = _mm1(s, w_ref[...]) + b_ref[...]


def _adaln(c_all, w_ada, b_ada):
    m, d = c_all.shape
    n = w_ada.shape[1]
    tn = _pick(n, (512, 256, 128))
    return pl.pallas_call(
        _adaln_kernel,
        grid=(n // tn,),
        in_specs=[pl.BlockSpec((m, d), lambda j: (0, 0)),
                  pl.BlockSpec((d, tn), lambda j: (0, j)),
                  pl.BlockSpec((1, tn), lambda j: (0, j))],
        out_specs=pl.BlockSpec((m, tn), lambda j: (0, j)),
        out_shape=jax.ShapeDtypeStruct((m, n), F32),
        compiler_params=_cparams(("parallel",)),
        name="adaln",
    )(c_all, w_ada, b_ada.reshape(1, n))


class _Mods:
    def __init__(self, arr, per_seq, d, rows_per_seq):
        self.arr, self.per_seq, self.d, self.rows_per_seq = arr, per_seq, d, rows_per_seq

    def spec(self, tm, tn, which, ij):
        off = which * self.d // tn
        if self.per_seq:
            tps = self.rows_per_seq // tm
            return pl.BlockSpec((1, 1, tn), lambda *g: (ij(*g)[0] // tps, 0, off + ij(*g)[1]))
        return pl.BlockSpec((tm, tn), lambda *g: (ij(*g)[0], off + ij(*g)[1]))


def _mod_val(ref):
    v = ref[...]
    return v[0] if v.ndim == 3 else v


def _norm_mod_kernel(x_ref, w_ref, sc_ref, sh_ref, o_ref):
    x = x_ref[...]
    y = x * lax.rsqrt(jnp.mean(x * x, axis=-1, keepdims=True) + NORM_EPS) * w_ref[...]
    o_ref[...] = (y * (1.0 + _mod_val(sc_ref)) + _mod_val(sh_ref)).astype(o_ref.dtype)


def _norm_mod(x, w, mods, sh_idx, sc_idx, tm):
    m, d = x.shape
    ij = lambda i: (i, 0)
    return pl.pallas_call(
        _norm_mod_kernel,
        grid=(m // tm,),
        in_specs=[pl.BlockSpec((tm, d), lambda i: (i, 0)),
                  pl.BlockSpec((1, d), lambda i: (0, 0)),
                  mods.spec(tm, d, sc_idx, ij),
                  mods.spec(tm, d, sh_idx, ij)],
        out_specs=pl.BlockSpec((tm, d), lambda i: (i, 0)),
        out_shape=jax.ShapeDtypeStruct((m, d), BF16),
        compiler_params=_cparams(("parallel",)),
        name="norm_mod",
    )(x, w.reshape(1, d), mods.arr, mods.arr)


def _mm_kernel(a_ref, w_ref, o_ref):
    o_ref[...] = _dot(a_ref[...], w_ref[...]).astype(o_ref.dtype)


def _matmul(a, w, tm, tn, out_dtype, name):
    m, k = a.shape
    n = w.shape[1]
    return pl.pallas_call(
        _mm_kernel,
        grid=(n // tn, m // tm),
        in_specs=[pl.BlockSpec((tm, k), lambda j, i: (i, 0)),
                  pl.BlockSpec((k, tn), lambda j, i: (0, j))],
        out_specs=pl.BlockSpec((tm, tn), lambda j, i: (i, j)),
        out_shape=jax.ShapeDtypeStruct((m, n), out_dtype),
        compiler_params=_cparams(("parallel", "parallel")),
        name=name,
    )(a, w)


def _relu2_kernel(a_ref, w_ref, o_ref):
    u = jnp.maximum(_dot(a_ref[...], w_ref[...]), 0.0)
    o_ref[...] = (u * u).astype(o_ref.dtype)


def _up_proj(a, w, tm, tn):
    m, k = a.shape
    n = w.shape[1]
    return pl.pallas_call(
        _relu2_kernel,
        grid=(n // tn, m // tm),
        in_specs=[pl.BlockSpec((tm, k), lambda j, i: (i, 0)),
                  pl.BlockSpec((k, tn), lambda j, i: (0, j))],
        out_specs=pl.BlockSpec((tm, tn), lambda j, i: (i, j)),
        out_shape=jax.ShapeDtypeStruct((m, n), BF16),
        compiler_params=_cparams(("parallel", "parallel")),
        name="up_proj",
    )(a, w)


def _merge_kernel(ya_ref, yb_ref, wa_ref, wb_ref, ga_ref, gb_ref, o_ref):
    pa = _dot(ya_ref[...], wa_ref[...])
    pb = _dot(yb_ref[...], wb_ref[...])
    o_ref[...] = (_sigmoid(ga_ref[...]) * pa + _sigmoid(gb_ref[...]) * pb).astype(o_ref.dtype)


def _merge(ya, yb, wa, wb, p_all, moff, tm, tn):
    m, ka = ya.shape
    kb = yb.shape[1]
    d = wa.shape[1]
    oa, ob = moff // tn, (moff + d) // tn
    return pl.pallas_call(
        _merge_kernel,
        grid=(d // tn, m // tm),
        in_specs=[pl.BlockSpec((tm, ka), lambda j, i: (i, 0)),
                  pl.BlockSpec((tm, kb), lambda j, i: (i, 0)),
                  pl.BlockSpec((ka, tn), lambda j, i: (0, j)),
                  pl.BlockSpec((kb, tn), lambda j, i: (0, j)),
                  pl.BlockSpec((tm, tn), lambda j, i: (i, oa + j)),
                  pl.BlockSpec((tm, tn), lambda j, i: (i, ob + j))],
        out_specs=pl.BlockSpec((tm, tn), lambda j, i: (i, j)),
        out_shape=jax.ShapeDtypeStruct((m, d), BF16),
        compiler_params=_cparams(("parallel", "parallel")),
        name="merge",
    )(ya, yb, wa, wb, p_all, p_all)


def _resid_kernel(a_ref, w_ref, x_ref, g_ref, o_ref):
    o_ref[...] = x_ref[...] + _mod_val(g_ref) * _dot(a_ref[...], w_ref[...])


def _out_resid(a, w, x, mods, gate_idx, tm, tn):
    m, k = a.shape
    n = w.shape[1]
    ij = lambda j, i: (i, j)
    return pl.pallas_call(
        _resid_kernel,
        grid=(n // tn, m // tm),
        in_specs=[pl.BlockSpec((tm, k), lambda j, i: (i, 0)),
                  pl.BlockSpec((k, tn), lambda j, i: (0, j)),
                  pl.BlockSpec((tm, tn), lambda j, i: (i, j)),
                  mods.spec(tm, tn, gate_idx, ij)],
        out_specs=pl.BlockSpec((tm, tn), lambda j, i: (i, j)),
        out_shape=jax.ShapeDtypeStruct((m, n), F32),
        compiler_params=_cparams(("parallel", "parallel")),
        name="out_resid",
    )(a, w, x, mods.arr)


def _down_kernel(a_ref, w_ref, x_ref, g_ref, fw_ref, o_ref):
    kk = pl.program_id(1)
    part = _dot(a_ref[...], w_ref[...])

    @pl.when(kk == 0)
    def _():
        o_ref[...] = part

    @pl.when(kk > 0)
    def _():
        o_ref[...] += part

    @pl.when(kk == pl.num_programs(1) - 1)
    def _():
        x2 = x_ref[...] + _mod_val(g_ref) * o_ref[...]
        y = x2 * lax.rsqrt(jnp.mean(x2 * x2, axis=-1, keepdims=True) + NORM_EPS)
        o_ref[...] = y * fw_ref[...]


def _down_final(a, w, x, mods, gate_idx, final_w, tm, tk):
    m, k = a.shape
    n = w.shape[1]
    ij = lambda i, kk: (i, 0)
    return pl.pallas_call(
        _down_kernel,
        grid=(m // tm, k // tk),
        in_specs=[pl.BlockSpec((tm, tk), lambda i, kk: (i, kk)),
                  pl.BlockSpec((tk, n), lambda i, kk: (kk, 0)),
                  pl.BlockSpec((tm, n), lambda i, kk: (i, 0), pipeline_mode=pl.Buffered(1)),
                  mods.spec(tm, n, gate_idx, ij),
                  pl.BlockSpec((1, n), lambda i, kk: (0, 0))],
        out_specs=pl.BlockSpec((tm, n), lambda i, kk: (i, 0)),
        out_shape=jax.ShapeDtypeStruct((m, n), F32),
        compiler_params=_cparams(("parallel", "arbitrary"), VMEM_LIMIT_WIDE),
        name="down_final",
    )(a, w, x, mods.arr, final_w.reshape(1, n))


def _rwkv_kernel(*refs, tb, sb, t_valid, has_state, levels):
    (r_ref, k_ref, v_ref, lo_ref, mur_ref, muk_ref, muv_ref, mul_ref,
     w0_ref, a0_ref, kk_ref, ka_ref, rk_ref, lw_ref, lb_ref,
     ww2_ref, wa2_ref, wg2_ref) = refs[:18]
    n = 18
    if has_state:
        s0_ref, shr_ref, shk_ref, shv_ref, shl_ref = refs[n:n + 5]
        n += 5
    ya_ref, sout_ref, s_ref, prev_ref = refs[n:n + 4]
    ci = pl.program_id(2)
    rows = sb * tb

    @pl.when(ci == 0)
    def _():
        if has_state:
            s_ref[...] = s0_ref[...]
            for slot, sh in enumerate((shr_ref, shk_ref, shv_ref, shl_ref)):
                for j in range(sb):
                    prev_ref[slot, j, 0:1, :] = sh[j]
        else:
            s_ref[...] = jnp.zeros_like(s_ref)
            prev_ref[...] = jnp.zeros_like(prev_ref)

    row = lax.broadcasted_iota(jnp.int32, (rows, 1), 0)
    is_first = row % tb == 0

    def tshift(x_ref, slot, mu_ref):
        x = x_ref[...]
        first = _rows_from([prev_ref[slot, j, 0:1, :] for j in range(sb)], tb)
        prev = jnp.where(is_first, first, pltpu.roll(x, 1, 0))
        for j in range(sb):
            prev_ref[slot, j, 0:1, :] = x[(j + 1) * tb - 1:(j + 1) * tb, :]
        return x + (prev - x) * mu_ref[...]

    r = tshift(r_ref, 0, mur_ref)
    k = tshift(k_ref, 1, muk_ref)
    v = tshift(v_ref, 2, muv_ref)
    lo = tshift(lo_ref, 3, mul_ref)
    dlw = ww2_ref.shape[0]
    dla = wa2_ref.shape[0]
    dlg = wg2_ref.shape[0]
    dw = lo[:, 0:dlw]
    da = lo[:, dlw:dlw + dla]
    dg = lo[:, dlw + dla:dlw + dla + dlg]

    log_w = -_softplus(-(w0_ref[...] + _mm3(jnp.tanh(dw), ww2_ref[...]))) - 0.5
    lw = -jnp.exp(log_w)
    a = _sigmoid(a0_ref[...] + _mm3(da, wa2_ref[...]))
    g = _mm3(_sigmoid(dg), wg2_ref[...])

    if t_valid is not None:
        valid = (row % tb < t_valid).astype(F32)
        lw = lw * valid
        k = k * valid
        v = v * valid

    cum = _mm_exact_lhs(_block_cumsum_matrix(rows, tb), lw)
    w_inc = jnp.exp(cum)
    w_inv = jnp.exp(-cum)
    rp = r * w_inc
    kx = k * kk_ref[...]
    k2 = k * (1.0 + (a - 1.0) * ka_ref[...])
    kp = k2 * w_inv
    rk2 = r * k2 * rk_ref[...]
    aw_exc = jnp.exp(cum - lw)
    aw_inv = a * w_inv

    strict2, causal2 = _tri_masks(tb, 2)
    nh = r.shape[1] // R_HEAD
    units = [(j, h) for j in range(sb) for h in range(nh)]

    def blk(x, j, h):
        return x[j * tb:(j + 1) * tb, h * R_HEAD:(h + 1) * R_HEAD]

    xs_, ys_, vs_ = [], [], []
    for j, h in units:
        kx_u = blk(kx, j, h)
        kk_u = kx_u * lax.rsqrt(jnp.sum(kx_u * kx_u, axis=-1, keepdims=True) + 1e-6)
        ap = -kk_u * blk(aw_exc, j, h)
        bp = kk_u * blk(aw_inv, j, h)
        xs_.append(jnp.concatenate([ap, blk(rp, j, h)], axis=0))
        ys_.append(jnp.concatenate([bp, blk(kp, j, h)], axis=0))
        vs_.append(blk(v, j, h))
    s_old = [s_ref[j, h] for j, h in units]
    amats = [_mm(x, y, NT, PASSES_MAIN) for x, y in zip(xs_, ys_)]
    xss = [_mm(x, s, NT, PASSES_MAIN) for x, s in zip(xs_, s_old)]
    tops = [jnp.where(strict2, am[:tb], 0.0) for am in amats]
    bots = [jnp.where(causal2, am[tb:], 0.0) for am in amats]
    tinvs = _inv_unit_lower([tp[:, :tb] for tp in tops], levels, R_PASSES_INV)
    rhs = [xs[:tb] + _mm(tp[:, tb:], vv, NN, PASSES_MAIN) for xs, tp, vv in zip(xss, tops, vs_)]
    us = [_mm(t, rr, NN, PASSES_MAIN) for t, rr in zip(tinvs, rhs)]
    uvs = [jnp.concatenate([u, vv], axis=0) for u, vv in zip(us, vs_)]
    os_ = [xs[tb:] + _mm(bt, uv, NN, PASSES_MAIN) for xs, bt, uv in zip(xss, bots, uvs)]
    upd = [_mm(uv, y, TN, PASSES_MAIN) for uv, y in zip(uvs, ys_)]

    lnw = lw_ref[...]
    lnb = lb_ref[...]
    out_rows = []
    for j in range(sb):
        outs = []
        for h in range(nh):
            i = j * nh + h
            wc = blk(w_inc, j, h)[tb - 1:tb, :]
            s_ref[j, h] = (s_old[i] + upd[i]) * wc
            o = os_[i]
            mu = jnp.mean(o, axis=-1, keepdims=True)
            var = jnp.mean(jnp.square(o - mu), axis=-1, keepdims=True)
            sl = slice(h * R_HEAD, (h + 1) * R_HEAD)
            on = (o - mu) * lax.rsqrt(var + RWKV_GN_EPS) * lnw[:, sl] + lnb[:, sl]
            bonus = jnp.sum(blk(rk2, j, h), axis=-1, keepdims=True) * vs_[i]
            outs.append(on + bonus)
        out_rows.append(jnp.concatenate(outs, axis=1))
    out = out_rows[0] if sb == 1 else jnp.concatenate(out_rows, axis=0)
    ya_ref[...] = (out * g).astype(ya_ref.dtype)

    @pl.when(ci == pl.num_programs(2) - 1)
    def _():
        sout_ref[...] = s_ref[...]


def _seq_block(nb, t_pad, nc):
    if nc > 1:
        return 1
    sb = max(1, SEQ_BLOCK_ROWS // t_pad)
    while nb % sb:
        sb -= 1
    return sb


def _rwkv(p_all, nb, t_pad, t_valid, rw, prm, state):
    tb = min(CHUNK, t_pad)
    nc = t_pad // tb
    sb = _seq_block(nb, t_pad, nc)
    rows = sb * tb
    nhb = rw // HBW
    heads = rw // R_HEAD
    kb, vb, lb = rw // HBW, 2 * rw // HBW, 3 * rw // HBW
    has_state = state is not None
    levels = max(1, (min(tb, t_valid or tb) - 1).bit_length())

    def pcol(off):
        return pl.BlockSpec((rows, HBW), lambda b, h, ci: (b * nc + ci, off + h))

    def vec(off):
        return pl.BlockSpec((1, HBW), lambda b, h, ci: (0, off + h))

    in_specs = [pcol(0), pcol(kb), pcol(vb),
                pl.BlockSpec((rows, HBW), lambda b, h, ci: (b * nc + ci, lb)),
                vec(0), vec(kb), vec(vb),
                pl.BlockSpec((1, HBW), lambda b, h, ci: (0, lb)),
                vec(0), vec(0), vec(0), vec(0), vec(0), vec(0), vec(0),
                pl.BlockSpec((prm["ww2"].shape[0], HBW), lambda b, h, ci: (0, h)),
                pl.BlockSpec((prm["wa2"].shape[0], HBW), lambda b, h, ci: (0, h)),
                pl.BlockSpec((prm["wg2"].shape[0], HBW), lambda b, h, ci: (0, h))]
    args = [p_all, p_all, p_all, p_all, prm["mu"], prm["mu"], prm["mu"], prm["mu"],
            prm["w0"], prm["a0"], prm["k_k"], prm["k_a"], prm["r_k"], prm["lnx_w"], prm["lnx_b"],
            prm["ww2"], prm["wa2"], prm["wg2"]]
    state_spec = pl.BlockSpec((sb, R_HB, R_HEAD, R_HEAD), lambda b, h, ci: (b, h, 0, 0))
    if has_state:
        s0, shift = state
        in_specs += [state_spec,
                     pl.BlockSpec((sb, 1, HBW), lambda b, h, ci: (b, 0, h)),
                     pl.BlockSpec((sb, 1, HBW), lambda b, h, ci: (b, 0, kb + h)),
                     pl.BlockSpec((sb, 1, HBW), lambda b, h, ci: (b, 0, vb + h)),
                     pl.BlockSpec((sb, 1, HBW), lambda b, h, ci: (b, 0, lb))]
        args += [s0, shift, shift, shift, shift]
    kern = functools.partial(_rwkv_kernel, tb=tb, sb=sb, t_valid=t_valid, has_state=has_state,
                             levels=levels)
    return pl.pallas_call(
        kern,
        grid=(nb // sb, nhb, nc),
        in_specs=in_specs,
        out_specs=[pl.BlockSpec((rows, HBW), lambda b, h, ci: (b * nc + ci, h)), state_spec],
        out_shape=[jax.ShapeDtypeStruct((nb * t_pad, rw), BF16),
                   jax.ShapeDtypeStruct((nb, heads, R_HEAD, R_HEAD), F32)],
        scratch_shapes=[pltpu.VMEM((sb, R_HB, R_HEAD, R_HEAD), F32),
                        pltpu.VMEM((4, sb, SUBLANES, HBW), F32)],
        compiler_params=_cparams(("parallel", "parallel", "arbitrary")),
        name="rwkv7_chunk",
    )(*args)


def _gdn_kernel(*refs, tb, sb, t_valid, has_state, levels):
    (q_ref, k_ref, v_ref, z_ref, ba_ref, cq_ref, ck_ref, cv_ref,
     an_ref, dt_ref, nw_ref) = refs[:11]
    n = 11
    if has_state:
        s0_ref, hq_ref, hk_ref, hv_ref = refs[n:n + 4]
        n += 4
    yb_ref, sout_ref, s_ref, hist_ref = refs[n:n + 4]
    ci = pl.program_id(2)
    rows = sb * tb

    @pl.when(ci == 0)
    def _():
        if has_state:
            s_ref[...] = s0_ref[...]
            hist_ref[0] = hq_ref[...]
            hist_ref[1] = hk_ref[...]
            hist_ref[2] = hv_ref[...]
        else:
            s_ref[...] = jnp.zeros_like(s_ref)
            hist_ref[...] = jnp.zeros_like(hist_ref)

    row = lax.broadcasted_iota(jnp.int32, (rows, 1), 0)

    def conv(x_ref, slot, cw_ref):
        x = x_ref[...]
        hists = [hist_ref[slot, j] for j in range(sb)]
        cw = cw_ref[...]
        acc = _shift_rows(x, hists, 3, tb) * cw[0:1, :]
        acc = acc + _shift_rows(x, hists, 2, tb) * cw[1:2, :]
        acc = acc + _shift_rows(x, hists, 1, tb) * cw[2:3, :]
        acc = acc + x * cw[3:4, :]
        for j in range(sb):
            hist_ref[slot, j] = x[(j + 1) * tb - SUBLANES:(j + 1) * tb, :]
        return _silu(acc)

    q = conv(q_ref, 0, cq_ref)
    k = conv(k_ref, 1, ck_ref)
    v = conv(v_ref, 2, cv_ref)
    z = z_ref[...]
    ba = ba_ref[...]
    beta_all = _sigmoid(ba)
    la_all = an_ref[...] * _softplus(ba + dt_ref[...])
    if t_valid is not None:
        valid = (row % tb < t_valid).astype(F32)
        beta_all = beta_all * valid
        la_all = la_all * valid
        q = q * valid
        k = k * valid
        v = v * valid

    g_all = _mm_exact_lhs(_block_cumsum_matrix(rows, tb), la_all)
    lrow = lax.broadcasted_iota(jnp.int32, (LANES, LANES), 0)
    lcol = lax.broadcasted_iota(jnp.int32, (LANES, LANES), 1)
    eye = jnp.where(lrow == lcol, 1.0, 0.0).astype(BF16)
    g_t = _mm_exact_lhs(eye, g_all, NT)
    eg_all = jnp.exp(g_all)
    strict, causal = _tri_masks(tb)
    nw = nw_ref[...]
    nh = q.shape[1] // G_HEAD
    units = [(j, h) for j in range(sb) for h in range(nh)]

    def blk(x, j, h):
        return x[j * tb:(j + 1) * tb, h * G_HEAD:(h + 1) * G_HEAD]

    qs, ks, kbs, rhs, decays, egs, gcols = [], [], [], [], [], [], []
    for j, h in units:
        rs = slice(j * tb, (j + 1) * tb)
        gcol = g_all[rs, G_HB + h:G_HB + h + 1]
        grow = g_t[G_HB + h:G_HB + h + 1, rs]
        beta = beta_all[rs, h:h + 1]
        eg = eg_all[rs, G_HB + h:G_HB + h + 1]
        decays.append(jnp.where(causal, jnp.exp(jnp.where(causal, gcol - grow, 0.0)), 0.0))
        q_u, k_u, v_u = blk(q, j, h), blk(k, j, h), blk(v, j, h)
        q_u = q_u * lax.rsqrt(jnp.sum(q_u * q_u, axis=-1, keepdims=True) + 1e-6) * (G_HEAD ** -0.5)
        k_u = k_u * lax.rsqrt(jnp.sum(k_u * k_u, axis=-1, keepdims=True) + 1e-6)
        kb = k_u * beta
        qs.append(q_u)
        ks.append(k_u)
        kbs.append(kb)
        rhs.append(jnp.concatenate([v_u * beta, kb * eg], axis=1))
        egs.append(eg)
        gcols.append(gcol)
    s_old = [s_ref[j, h] for j, h in units]
    amats = [_mm(jnp.concatenate([kb, q_u], axis=0), k_u, NT, PASSES_MAIN)
             for kb, q_u, k_u in zip(kbs, qs, ks)]
    lmats = [jnp.where(strict, am[:tb] * dc, 0.0) for am, dc in zip(amats, decays)]
    attns = [am[tb:] * dc for am, dc in zip(amats, decays)]
    tinvs = _inv_unit_lower([-lm for lm in lmats], levels, G_PASSES_INV)
    uws = [_mm(t, rr, NN, PASSES_MAIN) for t, rr in zip(tinvs, rhs)]
    xss = [_mm(jnp.concatenate([uw[:, G_HEAD:], q_u * eg], axis=0), s, NN, PASSES_MAIN)
           for uw, q_u, eg, s in zip(uws, qs, egs, s_old)]
    vnews = [uw[:, :G_HEAD] - xs[:tb] for uw, xs in zip(uws, xss)]
    os_ = [xs[tb:] + _mm(at, vn, NN, PASSES_MAIN) for xs, at, vn in zip(xss, attns, vnews)]
    ktails = [k_u * jnp.exp(gc[tb - 1:tb, :] - gc) for k_u, gc in zip(ks, gcols)]
    upd = [_mm(kt, vn, TN, PASSES_MAIN) for kt, vn in zip(ktails, vnews)]

    out_rows = []
    for j in range(sb):
        outs = []
        for h in range(nh):
            i = j * nh + h
            s_ref[j, h] = s_old[i] * egs[i][tb - 1:tb, :] + upd[i]
            o = os_[i]
            o = o * lax.rsqrt(jnp.mean(o * o, axis=-1, keepdims=True) + NORM_EPS) * nw
            outs.append(o)
        out_rows.append(jnp.concatenate(outs, axis=1))
    out = out_rows[0] if sb == 1 else jnp.concatenate(out_rows, axis=0)
    yb_ref[...] = (out * _silu(z)).astype(yb_ref.dtype)

    @pl.when(ci == pl.num_programs(2) - 1)
    def _():
        sout_ref[...] = s_ref[...]


def _gdn(p_all, nb, t_pad, t_valid, gw, goff, baoff, prm, state):
    tb = min(CHUNK, t_pad)
    nc = t_pad // tb
    sb = _seq_block(nb, t_pad, nc)
    rows = sb * tb
    nhb = gw // HBW
    heads = gw // G_HEAD
    qb = goff // HBW
    kb, vb, zb = qb + gw // HBW, qb + 2 * gw // HBW, qb + 3 * gw // HBW
    bab = baoff // LANES
    has_state = state is not None
    levels = max(1, (min(tb, t_valid or tb) - 1).bit_length())

    def pcol(off):
        return pl.BlockSpec((rows, HBW), lambda b, h, ci: (b * nc + ci, off + h))

    def cwspec(off):
        return pl.BlockSpec((G_CONV, HBW), lambda b, h, ci: (0, off + h))

    in_specs = [pcol(qb), pcol(kb), pcol(vb), pcol(zb),
                pl.BlockSpec((rows, LANES), lambda b, h, ci: (b * nc + ci, bab + h)),
                cwspec(0), cwspec(gw // HBW), cwspec(2 * gw // HBW),
                pl.BlockSpec((1, LANES), lambda b, h, ci: (0, h)),
                pl.BlockSpec((1, LANES), lambda b, h, ci: (0, h)),
                pl.BlockSpec((1, G_HEAD), lambda b, h, ci: (0, 0))]
    args = [p_all, p_all, p_all, p_all, p_all, prm["conv_w"], prm["conv_w"], prm["conv_w"],
            prm["neg_a"], prm["dt"], prm["norm_w"]]
    state_spec = pl.BlockSpec((sb, G_HB, G_HEAD, G_HEAD), lambda b, h, ci: (b, h, 0, 0))
    if has_state:
        s0, hist = state
        in_specs += [state_spec,
                     pl.BlockSpec((sb, SUBLANES, HBW), lambda b, h, ci: (b, 0, h)),
                     pl.BlockSpec((sb, SUBLANES, HBW), lambda b, h, ci: (b, 0, gw // HBW + h)),
                     pl.BlockSpec((sb, SUBLANES, HBW), lambda b, h, ci: (b, 0, 2 * gw // HBW + h))]
        args += [s0, hist, hist, hist]
    kern = functools.partial(_gdn_kernel, tb=tb, sb=sb, t_valid=t_valid, has_state=has_state,
                             levels=levels)
    return pl.pallas_call(
        kern,
        grid=(nb // sb, nhb, nc),
        in_specs=in_specs,
        out_specs=[pl.BlockSpec((rows, HBW), lambda b, h, ci: (b * nc + ci, h)), state_spec],
        out_shape=[jax.ShapeDtypeStruct((nb * t_pad, gw), BF16),
                   jax.ShapeDtypeStruct((nb, heads, G_HEAD, G_HEAD), F32)],
        scratch_shapes=[pltpu.VMEM((sb, G_HB, G_HEAD, G_HEAD), F32),
                        pltpu.VMEM((3, sb, SUBLANES, HBW), F32)],
        compiler_params=_cparams(("parallel", "parallel", "arbitrary")),
        name="gdn_chunk",
    )(*args)


def _run_group(x, mods, nb, t, lay, wts, rprm, gprm, states):
    m, d = x.shape
    rw, gw = lay["rw"], lay["gw"]
    tm = _pick(m, (512, 256, 128, 64, 32, 16, 8))
    if mods.per_seq:
        tm = _pick(t, (512, 256, 128, 64, 32, 16, 8))
    n_all = wts["w_all"].shape[1]
    tm_norm = min(tm, 256)

    h = _norm_mod(x, wts["norm1_w"], mods, 0, 1, tm_norm)
    p_all = _matmul(h, wts["w_all"], tm, _pick(n_all, (1024, 512, 256, 128)), F32, "in_proj")

    p3 = p_all.reshape(nb, t, n_all)
    shift_new = p3[:, t - 1, :lay["r_cols"]]
    conv_new = p3[:, t - (G_CONV - 1):, lay["goff"]:lay["goff"] + 3 * gw]

    if t % SUBLANES == 0:
        t_pad, t_valid, p_rec = t, None, p_all
    else:
        t_pad = -(-t // SUBLANES) * SUBLANES
        t_valid = t
        p_rec = jnp.pad(p3, ((0, 0), (0, t_pad - t), (0, 0))).reshape(nb * t_pad, n_all)

    ya, wkv_new = _rwkv(p_rec, nb, t_pad, t_valid, rw, rprm, states and states["rwkv"])
    yb, gdn_new = _gdn(p_rec, nb, t_pad, t_valid, gw, lay["goff"], lay["baoff"], gprm,
                       states and states["gdn"])
    if t_pad != t:
        ya = ya.reshape(nb, t_pad, rw)[:, :t].reshape(m, rw)
        yb = yb.reshape(nb, t_pad, gw)[:, :t].reshape(m, gw)

    tn_m = _pick(d, (1024, 512, 256, 128))
    while lay["moff"] % tn_m:
        tn_m //= 2
    merged = _merge(ya, yb, wts["w_out_a"], wts["w_out_b"], p_all, lay["moff"], tm, tn_m)
    x1 = _out_resid(merged, wts["w_out"], x, mods, 2, tm, _pick(d, (1024, 512, 256, 128)))
    h2 = _norm_mod(x1, wts["norm2_w"], mods, 3, 4, tm_norm)
    dff = wts["w_up"].shape[1]
    up = _up_proj(h2, wts["w_up"], tm, _pick(dff, (1024, 512, 256, 128)))
    y = _down_final(up, wts["w_down"], x1, mods, 5, wts["final_norm_w"], tm,
                    _pick(dff, (512, 256, 128)))
    return y, wkv_new, shift_new, gdn_new, conv_new


def kernel(x_prompt, x_sample, state_rwkv_wkv, state_rwkv_shift, state_gdn, state_gdn_conv,
           c_prompt, c_sample, norm1_w, norm2_w, w_ada, b_ada, w_in, r_mu, r_w0, r_w_w2, r_a0,
           r_w_a2, r_w_g2, r_k_k, r_k_a, r_r_k, r_lnx_w, r_lnx_b, g_conv_w, g_a_log, g_dt_bias,
           g_norm_w, w_out_a, w_out_b, w_out, w_up, w_down, final_norm_w):
    depth = w_in.shape[0]
    assert depth == 1, "single-layer trunk"
    bp, tp, d = x_prompt.shape
    bs, ts, _ = x_sample.shape
    rw = r_w0.shape[1]
    gw = g_conv_w.shape[2] // 3
    g_heads = g_a_log.shape[1]
    dlw, dla, dlg = r_w_w2.shape[1], r_w_a2.shape[1], r_w_g2.shape[1]
    lora = dlw + dla + dlg
    r_cols = 3 * rw + lora
    g_cols = 4 * gw + 2 * g_heads
    assert rw % HBW == 0 and gw % HBW == 0 and lora <= HBW and g_heads % G_HB == 0
    assert w_in.shape[2] == r_cols + g_cols + 2 * d

    rseg = 3 * rw + HBW
    goff = rseg
    baoff = goff + 4 * gw
    n_ba = (g_heads // G_HB) * LANES
    moff = baoff + n_ba
    lay = dict(rw=rw, gw=gw, r_cols=r_cols, goff=goff, baoff=baoff, moff=moff)

    wi = w_in[0]
    w_b = wi[:, r_cols + 4 * gw:r_cols + 4 * gw + g_heads].reshape(d, g_heads // G_HB, G_HB)
    w_a = wi[:, r_cols + 4 * gw + g_heads:r_cols + g_cols].reshape(d, g_heads // G_HB, G_HB)
    w_ba = jnp.concatenate([w_b, w_a, jnp.zeros((d, g_heads // G_HB, LANES - 2 * G_HB), F32)],
                           axis=2).reshape(d, n_ba)
    w_all = jnp.concatenate([wi[:, :r_cols], jnp.zeros((d, rseg - r_cols), F32),
                             wi[:, r_cols:r_cols + 4 * gw], w_ba,
                             wi[:, r_cols + g_cols:]], axis=1).astype(BF16)

    wts = dict(w_all=w_all, norm1_w=norm1_w[0], norm2_w=norm2_w[0],
               w_out_a=w_out_a[0].astype(BF16), w_out_b=w_out_b[0].astype(BF16),
               w_out=w_out[0].astype(BF16), w_up=w_up[0].astype(BF16),
               w_down=w_down[0].astype(BF16), final_norm_w=final_norm_w)

    row = lambda a: a.reshape(1, -1)
    rprm = dict(mu=jnp.pad(row(r_mu[0]), ((0, 0), (0, rseg - r_cols))),
                w0=row(r_w0[0]), a0=row(r_a0[0]), k_k=row(r_k_k[0]), k_a=row(r_k_a[0]),
                r_k=row(r_r_k[0]), lnx_w=row(r_lnx_w[0]), lnx_b=row(r_lnx_b[0]),
                ww2=r_w_w2[0], wa2=r_w_a2[0], wg2=r_w_g2[0])

    def ba_row(vals):
        v = vals.reshape(g_heads // G_HB, G_HB)
        z = jnp.zeros_like(v)
        return jnp.concatenate([z, v, jnp.zeros((g_heads // G_HB, LANES - 2 * G_HB), F32)],
                               axis=1).reshape(1, n_ba)

    gprm = dict(conv_w=g_conv_w[0], neg_a=ba_row(-jnp.exp(g_a_log[0])), dt=ba_row(g_dt_bias[0]),
                norm_w=row(g_norm_w[0]))

    c_all = jnp.concatenate([c_prompt, c_sample], axis=0)
    n_seq = bp + bs
    c_all = jnp.pad(c_all, ((0, -n_seq % SUBLANES), (0, 0)))
    mods_all = _adaln(c_all, w_ada[0], b_ada[0])
    mods_p = _Mods(mods_all[:bp].reshape(bp, 1, 6 * d), True, d, tp)
    mods_s = _Mods(jnp.repeat(mods_all[bp:n_seq], ts, axis=0), False, d, ts)

    y_p, p_wkv, p_shift, p_gdn, p_conv = _run_group(
        x_prompt.reshape(bp * tp, d), mods_p, bp, tp, lay, wts, rprm, gprm, None)

    shift0 = jnp.pad(state_rwkv_shift[0], ((0, 0), (0, rseg - r_cols))).reshape(bs, 1, rseg)
    hist0 = jnp.pad(state_gdn_conv[0], ((0, 0), (SUBLANES - (G_CONV - 1), 0), (0, 0)))
    states = dict(rwkv=(state_rwkv_wkv[0], shift0), gdn=(state_gdn[0], hist0))
    y_s, s_wkv, s_shift, s_gdn, s_conv = _run_group(
        x_sample.reshape(bs * ts, d), mods_s, bs, ts, lay, wts, rprm, gprm, states)

    return (y_p.reshape(bp, tp, d), y_s.reshape(bs, ts, d),
            p_wkv[None], p_shift[None], p_gdn[None], p_conv[None],
            s_wkv[None], s_shift[None], s_gdn[None], s_conv[None])
```

```python
import functools

import jax
import jax.numpy as jnp
from jax import lax
from jax.experimental import pallas as pl
from jax.experimental.pallas import tpu as pltpu

F32 = jnp.float32
BF16 = jnp.bfloat16

LANES = 128
SUBLANES = 8
VMEM_LIMIT = 48 * 1024 * 1024
VMEM_LIMIT_WIDE = 56 * 1024 * 1024

R_HEAD = 64
G_HEAD = 128
G_CONV = 4
R_HB = 8
G_HB = 4
HBW = 512
CHUNK = 64
SEQ_BLOCK_ROWS = 64
R_LONG_SEQ_BLOCK = 2
G_LONG_SEQ_BLOCK = 2
NORM_EPS = 1e-6
RWKV_GN_EPS = 64e-5
PASSES_MAIN = 1
R_PASSES_INV = 3
G_PASSES_INV = 3

NN = ((1,), (0,))
NT = ((1,), (1,))
TN = ((0,), (0,))


def _cparams(sem, vmem=VMEM_LIMIT):
    return pltpu.CompilerParams(dimension_semantics=sem, vmem_limit_bytes=vmem)


def _pick(n, cands):
    for c in cands:
        if n % c == 0:
            return c
    raise ValueError(f"no tile in {cands} divides {n}")


def _dot(a, b, dims=NN):
    return lax.dot_general(a, b, (dims, ((), ())), preferred_element_type=F32)


def _mm1(a, b, dims=NN):
    return _dot(a.astype(BF16), b.astype(BF16), dims)


def _split2(a):
    hi = a.astype(BF16)
    lo = (a - hi.astype(F32)).astype(BF16)
    return hi, lo


def _mm3(a, b, dims=NN):
    ah, al = _split2(a)
    bh, bl = _split2(b)
    if dims == TN:
        return _dot(ah, bh, dims) + (_dot(ah, bl, dims) + _dot(al, bh, dims))
    m = a.shape[0]
    ah32 = ah.astype(F32)
    stacked = jnp.concatenate([ah32, a - ah32], axis=0).astype(BF16)
    r = _dot(stacked, bh, dims)
    return r[:m] + (_dot(ah, bl, dims) + r[m:])


def _mm(a, b, dims=NN, passes=1):
    return _mm1(a, b, dims) if passes == 1 else _mm3(a, b, dims)


def _mm3_presplit(a, b_hl):
    m = a.shape[0]
    kdim = b_hl.shape[0] // 2
    ah = a.astype(BF16)
    ah32 = ah.astype(F32)
    stacked = jnp.concatenate([ah32, a - ah32], axis=0).astype(BF16)
    r = _dot(stacked, b_hl[:kdim])
    return r[:m] + (_dot(ah, b_hl[kdim:]) + r[m:])


def _mm_exact_lhs(a_bf16, b, dims=NN):
    b1 = b.astype(BF16)
    r1 = b - b1.astype(F32)
    b2 = r1.astype(BF16)
    b3 = (r1 - b2.astype(F32)).astype(BF16)
    return _dot(a_bf16, b1, dims) + (_dot(a_bf16, b2, dims) + _dot(a_bf16, b3, dims))


def _sigmoid(x):
    return 1.0 / (1.0 + jnp.exp(-x))


def _silu(x):
    return x * _sigmoid(x)


def _softplus(x):
    return jnp.maximum(x, 0.0) + jnp.log1p(jnp.exp(-jnp.abs(x)))


def _tri_masks(c, reps=1):
    row = lax.broadcasted_iota(jnp.int32, (c, reps * c), 0)
    col = lax.broadcasted_iota(jnp.int32, (c, reps * c), 1) % c
    return row > col, row >= col


def _block_cumsum_matrix(rows, tb):
    row = lax.broadcasted_iota(jnp.int32, (rows, rows), 0)
    col = lax.broadcasted_iota(jnp.int32, (rows, rows), 1)
    keep = (row >= col) & (row // tb == col // tb)
    return jnp.where(keep, 1.0, 0.0).astype(BF16)


def _inv_unit_lower(ns, levels, passes):
    c = ns[0].shape[0]
    row = lax.broadcasted_iota(jnp.int32, (c, c), 0)
    col = lax.broadcasted_iota(jnp.int32, (c, c), 1)
    eye = jnp.where(row == col, 1.0, 0.0).astype(F32)
    ts = [eye + n for n in ns]
    if levels == 1:
        return ts
    ps = [_mm(n, n, NN, passes) for n in ns]
    for lvl in range(1, levels):
        if lvl == levels - 1:
            ts = [t + _mm(t, p, NN, passes) for t, p in zip(ts, ps)]
        else:
            rs = [_mm(jnp.concatenate([t, p], axis=0), p, NN, passes) for t, p in zip(ts, ps)]
            ts = [t + r[:c] for t, r in zip(ts, rs)]
            ps = [r[c:] for r in rs]
    return ts


def _shift_rows(x, hists, k, tb):
    rows = x.shape[0]
    xr = pltpu.roll(x, k, 0)
    hr = [pltpu.roll(h, k, 0) for h in hists]
    if tb == SUBLANES:
        hfull = hr[0] if len(hr) == 1 else jnp.concatenate(hr, axis=0)
        row = lax.broadcasted_iota(jnp.int32, (rows, 1), 0)
        return jnp.where(row % tb < k, hfull, xr)
    row8 = lax.broadcasted_iota(jnp.int32, (SUBLANES, 1), 0)
    pieces = []
    for j, h in enumerate(hr):
        pieces.append(jnp.where(row8 < k, h, xr[j * tb:j * tb + SUBLANES]))
        pieces.append(xr[j * tb + SUBLANES:(j + 1) * tb])
    return jnp.concatenate(pieces, axis=0)


def _rows_from(parts, tb):
    tiles = [jnp.broadcast_to(p, (tb, p.shape[1])) for p in parts]
    return tiles[0] if len(tiles) == 1 else jnp.concatenate(tiles, axis=0)


def _adaln_kernel(c_ref, w_ref, b_ref, o_ref):
    c = c_ref[...]
    s = _silu(c)
    o_ref[...] = _mm1(s, w_ref[...]) + b_ref[...]


def _adaln(c_all, w_ada, b_ada):
    m, d = c_all.shape
    n = w_ada.shape[1]
    tn = _pick(n, (512, 256, 128))
    return pl.pallas_call(
        _adaln_kernel,
        grid=(n // tn,),
        in_specs=[pl.BlockSpec((m, d), lambda j: (0, 0)),
                  pl.BlockSpec((d, tn), lambda j: (0, j)),
                  pl.BlockSpec((1, tn), lambda j: (0, j))],
        out_specs=pl.BlockSpec((m, tn), lambda j: (0, j)),
        out_shape=jax.ShapeDtypeStruct((m, n), F32),
        compiler_params=_cparams(("parallel",)),
        name="adaln",
    )(c_all, w_ada, b_ada.reshape(1, n))


class _Mods:
    def __init__(self, arr, per_seq, d, rows_per_seq):
        self.arr, self.per_seq, self.d, self.rows_per_seq = arr, per_seq, d, rows_per_seq

    def spec(self, tm, tn, which, ij):
        off = which * self.d // tn
        if self.per_seq:
            tps = self.rows_per_seq // tm
            return pl.BlockSpec((1, 1, tn), lambda *g: (ij(*g)[0] // tps, 0, off + ij(*g)[1]))
        return pl.BlockSpec((tm, tn), lambda *g: (ij(*g)[0], off + ij(*g)[1]))


def _mod_val(ref):
    v = ref[...]
    return v[0] if v.ndim == 3 else v


def _norm_mod_kernel(x_ref, w_ref, sc_ref, sh_ref, o_ref):
    x = x_ref[...]
    y = x * lax.rsqrt(jnp.mean(x * x, axis=-1, keepdims=True) + NORM_EPS) * w_ref[...]
    o_ref[...] = (y * (1.0 + _mod_val(sc_ref)) + _mod_val(sh_ref)).astype(o_ref.dtype)


def _norm_mod(x, w, mods, sh_idx, sc_idx, tm):
    m, d = x.shape
    ij = lambda i: (i, 0)
    return pl.pallas_call(
        _norm_mod_kernel,
        grid=(m // tm,),
        in_specs=[pl.BlockSpec((tm, d), lambda i: (i, 0)),
                  pl.BlockSpec((1, d), lambda i: (0, 0)),
                  mods.spec(tm, d, sc_idx, ij),
                  mods.spec(tm, d, sh_idx, ij)],
        out_specs=pl.BlockSpec((tm, d), lambda i: (i, 0)),
        out_shape=jax.ShapeDtypeStruct((m, d), BF16),
        compiler_params=_cparams(("parallel",)),
        name="norm_mod",
    )(x, w.reshape(1, d), mods.arr, mods.arr)


def _mm_kernel(a_ref, w_ref, o_ref):
    o_ref[...] = _dot(a_ref[...], w_ref[...]).astype(o_ref.dtype)


def _matmul(a, w, tm, tn, out_dtype, name):
    m, k = a.shape
    n = w.shape[1]
    return pl.pallas_call(
        _mm_kernel,
        grid=(n // tn, m // tm),
        in_specs=[pl.BlockSpec((tm, k), lambda j, i: (i, 0)),
                  pl.BlockSpec((k, tn), lambda j, i: (0, j))],
        out_specs=pl.BlockSpec((tm, tn), lambda j, i: (i, j)),
        out_shape=jax.ShapeDtypeStruct((m, n), out_dtype),
        compiler_params=_cparams(("parallel", "parallel")),
        name=name,
    )(a, w)


def _relu2_kernel(a_ref, w_ref, o_ref):
    u = jnp.maximum(_dot(a_ref[...], w_ref[...]), 0.0)
    o_ref[...] = (u * u).astype(o_ref.dtype)


def _up_proj(a, w, tm, tn):
    m, k = a.shape
    n = w.shape[1]
    return pl.pallas_call(
        _relu2_kernel,
        grid=(n // tn, m // tm),
        in_specs=[pl.BlockSpec((tm, k), lambda j, i: (i, 0)),
                  pl.BlockSpec((k, tn), lambda j, i: (0, j))],
        out_specs=pl.BlockSpec((tm, tn), lambda j, i: (i, j)),
        out_shape=jax.ShapeDtypeStruct((m, n), BF16),
        compiler_params=_cparams(("parallel", "parallel")),
        name="up_proj",
    )(a, w)


def _merge_kernel(ya_ref, yb_ref, wa_ref, wb_ref, ga_ref, gb_ref, o_ref):
    pa = _dot(ya_ref[...], wa_ref[...])
    pb = _dot(yb_ref[...], wb_ref[...])
    o_ref[...] = (_sigmoid(ga_ref[...]) * pa + _sigmoid(gb_ref[...]) * pb).astype(o_ref.dtype)


def _merge(ya, yb, wa, wb, p_all, moff, tm, tn):
    m, ka = ya.shape
    kb = yb.shape[1]
    d = wa.shape[1]
    oa, ob = moff // tn, (moff + d) // tn
    return pl.pallas_call(
        _merge_kernel,
        grid=(d // tn, m // tm),
        in_specs=[pl.BlockSpec((tm, ka), lambda j, i: (i, 0)),
                  pl.BlockSpec((tm, kb), lambda j, i: (i, 0)),
                  pl.BlockSpec((ka, tn), lambda j, i: (0, j)),
                  pl.BlockSpec((kb, tn), lambda j, i: (0, j)),
                  pl.BlockSpec((tm, tn), lambda j, i: (i, oa + j)),
                  pl.BlockSpec((tm, tn), lambda j, i: (i, ob + j))],
        out_specs=pl.BlockSpec((tm, tn), lambda j, i: (i, j)),
        out_shape=jax.ShapeDtypeStruct((m, d), BF16),
        compiler_params=_cparams(("parallel", "parallel")),
        name="merge",
    )(ya, yb, wa, wb, p_all, p_all)


def _resid_kernel(a_ref, w_ref, x_ref, g_ref, o_ref):
    o_ref[...] = x_ref[...] + _mod_val(g_ref) * _dot(a_ref[...], w_ref[...])


def _out_resid(a, w, x, mods, gate_idx, tm, tn):
    m, k = a.shape
    n = w.shape[1]
    ij = lambda j, i: (i, j)
    return pl.pallas_call(
        _resid_kernel,
        grid=(n // tn, m // tm),
        in_specs=[pl.BlockSpec((tm, k), lambda j, i: (i, 0)),
                  pl.BlockSpec((k, tn), lambda j, i: (0, j)),
                  pl.BlockSpec((tm, tn), lambda j, i: (i, j)),
                  mods.spec(tm, tn, gate_idx, ij)],
        out_specs=pl.BlockSpec((tm, tn), lambda j, i: (i, j)),
        out_shape=jax.ShapeDtypeStruct((m, n), F32),
        compiler_params=_cparams(("parallel", "parallel")),
        name="out_resid",
    )(a, w, x, mods.arr)


def _down_kernel(a_ref, w_ref, x_ref, g_ref, fw_ref, o_ref):
    kk = pl.program_id(1)

    @pl.when(kk == 0)
    def _():
        o_ref[...] = jnp.zeros_like(o_ref)

    o_ref[...] = _dot(a_ref[...], w_ref[...]) + o_ref[...]

    @pl.when(kk == pl.num_programs(1) - 1)
    def _():
        x2 = x_ref[...] + _mod_val(g_ref) * o_ref[...]
        y = x2 * lax.rsqrt(jnp.mean(x2 * x2, axis=-1, keepdims=True) + NORM_EPS)
        o_ref[...] = y * fw_ref[...]


def _down_final(a, w, x, mods, gate_idx, final_w, tm, tk):
    m, k = a.shape
    n = w.shape[1]
    ij = lambda i, kk: (i, 0)
    return pl.pallas_call(
        _down_kernel,
        grid=(m // tm, k // tk),
        in_specs=[pl.BlockSpec((tm, tk), lambda i, kk: (i, kk)),
                  pl.BlockSpec((tk, n), lambda i, kk: (kk, 0)),
                  pl.BlockSpec((tm, n), lambda i, kk: (i, 0), pipeline_mode=pl.Buffered(1)),
                  mods.spec(tm, n, gate_idx, ij),
                  pl.BlockSpec((1, n), lambda i, kk: (0, 0))],
        out_specs=pl.BlockSpec((tm, n), lambda i, kk: (i, 0), pipeline_mode=pl.Buffered(1)),
        out_shape=jax.ShapeDtypeStruct((m, n), F32),
        compiler_params=_cparams(("parallel", "arbitrary"), VMEM_LIMIT_WIDE),
        name="down_final",
    )(a, w, x, mods.arr, final_w.reshape(1, n))


def _rwkv_kernel(*refs, tb, sb, t_valid, has_state, levels):
    (r_ref, k_ref, v_ref, lo_ref, mur_ref, muk_ref, muv_ref, mul_ref,
     w0_ref, a0_ref, kk_ref, ka_ref, rk_ref, lw_ref, lb_ref,
     ww2_ref, wa2_ref, wg2_ref) = refs[:18]
    n = 18
    if has_state:
        s0_ref, shr_ref, shk_ref, shv_ref, shl_ref = refs[n:n + 5]
        n += 5
    ya_ref, sout_ref, s_ref, prev_ref = refs[n:n + 4]
    ci = pl.program_id(2)
    rows = sb * tb

    @pl.when(ci == 0)
    def _():
        if has_state:
            s_ref[...] = s0_ref[...]
            for slot, sh in enumerate((shr_ref, shk_ref, shv_ref, shl_ref)):
                for j in range(sb):
                    prev_ref[slot, j, 0:1, :] = sh[j]
        else:
            s_ref[...] = jnp.zeros_like(s_ref)
            prev_ref[...] = jnp.zeros_like(prev_ref)

    row = lax.broadcasted_iota(jnp.int32, (rows, 1), 0)
    is_first = row % tb == 0

    def tshift(x_ref, slot, mu_ref):
        x = x_ref[...].reshape(rows, HBW)
        first = _rows_from([prev_ref[slot, j, 0:1, :] for j in range(sb)], tb)
        prev = jnp.where(is_first, first, pltpu.roll(x, 1, 0))
        for j in range(sb):
            prev_ref[slot, j, 0:1, :] = x[(j + 1) * tb - 1:(j + 1) * tb, :]
        return x + (prev - x) * mu_ref[...]

    r = tshift(r_ref, 0, mur_ref)
    k = tshift(k_ref, 1, muk_ref)
    v = tshift(v_ref, 2, muv_ref)
    lo = tshift(lo_ref, 3, mul_ref)
    dlw = ww2_ref.shape[0] // 2
    dla = wa2_ref.shape[0] // 2
    dlg = wg2_ref.shape[0] // 2
    dw = lo[:, 0:dlw]
    da = lo[:, dlw:dlw + dla]
    dg = lo[:, dlw + dla:dlw + dla + dlg]

    log_w = -_softplus(-(w0_ref[...] + _mm3_presplit(jnp.tanh(dw), ww2_ref[...]))) - 0.5
    lw = -jnp.exp(log_w)
    a = _sigmoid(a0_ref[...] + _mm3_presplit(da, wa2_ref[...]))
    g = _mm3_presplit(_sigmoid(dg), wg2_ref[...])

    if t_valid is not None:
        valid = (row % tb < t_valid).astype(F32)
        lw = lw * valid
        k = k * valid
        v = v * valid

    cum = _mm_exact_lhs(_block_cumsum_matrix(rows, tb), lw)
    w_inc = jnp.exp(cum)
    w_inv = jnp.exp(-cum)
    rp = r * w_inc
    kx = k * kk_ref[...]
    k2 = k * (1.0 + (a - 1.0) * ka_ref[...])
    kp = k2 * w_inv
    rk2 = r * k2 * rk_ref[...]
    aw_exc = jnp.exp(cum - lw)
    aw_inv = a * w_inv

    strict2, causal2 = _tri_masks(tb, 2)
    nh = r.shape[1] // R_HEAD
    units = [(j, h) for j in range(sb) for h in range(nh)]

    def blk(x, j, h):
        return x[j * tb:(j + 1) * tb, h * R_HEAD:(h + 1) * R_HEAD]

    xs_, ys_, vs_ = [], [], []
    for j, h in units:
        kx_u = blk(kx, j, h)
        kk_u = kx_u * lax.rsqrt(jnp.sum(kx_u * kx_u, axis=-1, keepdims=True) + 1e-6)
        ap = -kk_u * blk(aw_exc, j, h)
        bp = kk_u * blk(aw_inv, j, h)
        xs_.append(jnp.concatenate([ap, blk(rp, j, h)], axis=0))
        ys_.append(jnp.concatenate([bp, blk(kp, j, h)], axis=0))
        vs_.append(blk(v, j, h))
    s_old = [s_ref[j, h] for j, h in units]
    amats = [_mm(x, y, NT, PASSES_MAIN) for x, y in zip(xs_, ys_)]
    xss = [_mm(x, s, NT, PASSES_MAIN) for x, s in zip(xs_, s_old)]
    tops = [jnp.where(strict2, am[:tb], 0.0) for am in amats]
    bots = [jnp.where(causal2, am[tb:], 0.0) for am in amats]
    tinvs = _inv_unit_lower([tp[:, :tb] for tp in tops], levels, R_PASSES_INV)
    rhs = [xs[:tb] + _mm(tp[:, tb:], vv, NN, PASSES_MAIN) for xs, tp, vv in zip(xss, tops, vs_)]
    us = [_mm(t, rr, NN, PASSES_MAIN) for t, rr in zip(tinvs, rhs)]
    uvs = [jnp.concatenate([u, vv], axis=0) for u, vv in zip(us, vs_)]
    os_ = [xs[tb:] + _mm(bt, uv, NN, PASSES_MAIN) for xs, bt, uv in zip(xss, bots, uvs)]
    upd = [_mm(uv, y, TN, PASSES_MAIN) for uv, y in zip(uvs, ys_)]

    lnw = lw_ref[...]
    lnb = lb_ref[...]
    out_rows = []
    for j in range(sb):
        outs = []
        for h in range(nh):
            i = j * nh + h
            wc = blk(w_inc, j, h)[tb - 1:tb, :]
            s_ref[j, h] = (s_old[i] + upd[i]) * wc
            o = os_[i]
            mu = jnp.mean(o, axis=-1, keepdims=True)
            var = jnp.mean(jnp.square(o - mu), axis=-1, keepdims=True)
            sl = slice(h * R_HEAD, (h + 1) * R_HEAD)
            on = (o - mu) * lax.rsqrt(var + RWKV_GN_EPS) * lnw[:, sl] + lnb[:, sl]
            bonus = jnp.sum(blk(rk2, j, h), axis=-1, keepdims=True) * vs_[i]
            outs.append(on + bonus)
        out_rows.append(jnp.concatenate(outs, axis=1))
    out = out_rows[0] if sb == 1 else jnp.concatenate(out_rows, axis=0)
    ya_ref[...] = (out * g).reshape(sb, tb, HBW).astype(ya_ref.dtype)

    @pl.when(ci == pl.num_programs(2) - 1)
    def _():
        sout_ref[...] = s_ref[...]


def _seq_block(nb, t_pad, nc, long_seq_block):
    sb = long_seq_block if nc > 1 else max(1, SEQ_BLOCK_ROWS // t_pad)
    while nb % sb:
        sb -= 1
    return sb


def _rwkv(p_all, nb, t_pad, t_valid, rw, prm, state):
    tb = min(CHUNK, t_pad)
    nc = t_pad // tb
    sb = _seq_block(nb, t_pad, nc, R_LONG_SEQ_BLOCK)
    nhb = rw // HBW
    heads = rw // R_HEAD
    kb, vb, lb = rw // HBW, 2 * rw // HBW, 3 * rw // HBW
    has_state = state is not None
    levels = max(1, (min(tb, t_valid or tb) - 1).bit_length())
    ya_dtype = BF16 if tb % (2 * SUBLANES) == 0 else F32

    def pcol(off):
        return pl.BlockSpec((sb, tb, HBW), lambda b, h, ci: (b, ci, off + h))

    def vec(off):
        return pl.BlockSpec((1, HBW), lambda b, h, ci: (0, off + h))

    in_specs = [pcol(0), pcol(kb), pcol(vb),
                pl.BlockSpec((sb, tb, HBW), lambda b, h, ci: (b, ci, lb)),
                vec(0), vec(kb), vec(vb),
                pl.BlockSpec((1, HBW), lambda b, h, ci: (0, lb)),
                vec(0), vec(0), vec(0), vec(0), vec(0), vec(0), vec(0),
                pl.BlockSpec((prm["ww2"].shape[0], HBW), lambda b, h, ci: (0, h)),
                pl.BlockSpec((prm["wa2"].shape[0], HBW), lambda b, h, ci: (0, h)),
                pl.BlockSpec((prm["wg2"].shape[0], HBW), lambda b, h, ci: (0, h))]
    args = [p_all, p_all, p_all, p_all, prm["mu"], prm["mu"], prm["mu"], prm["mu"],
            prm["w0"], prm["a0"], prm["k_k"], prm["k_a"], prm["r_k"], prm["lnx_w"], prm["lnx_b"],
            prm["ww2"], prm["wa2"], prm["wg2"]]
    state_spec = pl.BlockSpec((sb, R_HB, R_HEAD, R_HEAD), lambda b, h, ci: (b, h, 0, 0))
    if has_state:
        s0, shift = state
        in_specs += [state_spec,
                     pl.BlockSpec((sb, 1, HBW), lambda b, h, ci: (b, 0, h)),
                     pl.BlockSpec((sb, 1, HBW), lambda b, h, ci: (b, 0, kb + h)),
                     pl.BlockSpec((sb, 1, HBW), lambda b, h, ci: (b, 0, vb + h)),
                     pl.BlockSpec((sb, 1, HBW), lambda b, h, ci: (b, 0, lb))]
        args += [s0, shift, shift, shift, shift]
    kern = functools.partial(_rwkv_kernel, tb=tb, sb=sb, t_valid=t_valid, has_state=has_state,
                             levels=levels)
    return pl.pallas_call(
        kern,
        grid=(nb // sb, nhb, nc),
        in_specs=in_specs,
        out_specs=[pl.BlockSpec((sb, tb, HBW), lambda b, h, ci: (b, ci, h)), state_spec],
        out_shape=[jax.ShapeDtypeStruct((nb, t_pad, rw), ya_dtype),
                   jax.ShapeDtypeStruct((nb, heads, R_HEAD, R_HEAD), F32)],
        scratch_shapes=[pltpu.VMEM((sb, R_HB, R_HEAD, R_HEAD), F32),
                        pltpu.VMEM((4, sb, SUBLANES, HBW), F32)],
        compiler_params=_cparams(("parallel", "parallel", "arbitrary")),
        name="rwkv7_chunk",
    )(*args)


def _gdn_kernel(*refs, tb, sb, t_valid, has_state, levels):
    (q_ref, k_ref, v_ref, z_ref, ba_ref, cq_ref, ck_ref, cv_ref,
     an_ref, dt_ref, nw_ref) = refs[:11]
    n = 11
    if has_state:
        s0_ref, hq_ref, hk_ref, hv_ref = refs[n:n + 4]
        n += 4
    yb_ref, sout_ref, s_ref, hist_ref = refs[n:n + 4]
    ci = pl.program_id(2)
    rows = sb * tb

    @pl.when(ci == 0)
    def _():
        if has_state:
            s_ref[...] = s0_ref[...]
            hist_ref[0] = hq_ref[...]
            hist_ref[1] = hk_ref[...]
            hist_ref[2] = hv_ref[...]
        else:
            s_ref[...] = jnp.zeros_like(s_ref)
            hist_ref[...] = jnp.zeros_like(hist_ref)

    row = lax.broadcasted_iota(jnp.int32, (rows, 1), 0)

    def conv(x_ref, slot, cw_ref):
        x = x_ref[...].reshape(rows, HBW)
        hists =[hist_ref[slot, j] for j in range(sb)]
        cw = cw_ref[...]
        acc = _shift_rows(x, hists, 3, tb) * cw[0:1, :]
        acc = acc + _shift_rows(x, hists, 2, tb) * cw[1:2, :]
        acc = acc + _shift_rows(x, hists, 1, tb) * cw[2:3, :]
        acc = acc + x * cw[3:4, :]
        for j in range(sb):
            hist_ref[slot, j] = x[(j + 1) * tb - SUBLANES:(j + 1) * tb, :]
        return _silu(acc)

    q = conv(q_ref, 0, cq_ref)
    k = conv(k_ref, 1, ck_ref)
    v = conv(v_ref, 2, cv_ref)
    z = z_ref[...].reshape(rows, HBW)
    ba = ba_ref[...].reshape(rows, LANES)
    beta_all = _sigmoid(ba)
    la_all = an_ref[...] * _softplus(ba + dt_ref[...])
    if t_valid is not None:
        valid = (row % tb < t_valid).astype(F32)
        beta_all = beta_all * valid
        la_all = la_all * valid
        q = q * valid
        k = k * valid
        v = v * valid

    g_all = _mm_exact_lhs(_block_cumsum_matrix(rows, tb), la_all)
    lrow = lax.broadcasted_iota(jnp.int32, (LANES, LANES), 0)
    lcol = lax.broadcasted_iota(jnp.int32, (LANES, LANES), 1)
    eye = jnp.where(lrow == lcol, 1.0, 0.0).astype(BF16)
    g_t = _mm_exact_lhs(eye, g_all, NT)
    eg_all = jnp.exp(g_all)
    strict, causal = _tri_masks(tb)
    nw = nw_ref[...]
    nh = q.shape[1] // G_HEAD
    units = [(j, h) for j in range(sb) for h in range(nh)]

    def blk(x, j, h):
        return x[j * tb:(j + 1) * tb, h * G_HEAD:(h + 1) * G_HEAD]

    qs, ks, kbs, rhs, decays, egs, gcols = [], [], [], [], [], [], []
    for j, h in units:
        rs = slice(j * tb, (j + 1) * tb)
        gcol = g_all[rs, G_HB + h:G_HB + h + 1]
        grow = g_t[G_HB + h:G_HB + h + 1, rs]
        beta = beta_all[rs, h:h + 1]
        eg = eg_all[rs, G_HB + h:G_HB + h + 1]
        decays.append(jnp.where(causal, jnp.exp(jnp.where(causal, gcol - grow, 0.0)), 0.0))
        q_u, k_u, v_u = blk(q, j, h), blk(k, j, h), blk(v, j, h)
        q_u = q_u * lax.rsqrt(jnp.sum(q_u * q_u, axis=-1, keepdims=True) + 1e-6) * (G_HEAD ** -0.5)
        k_u = k_u * lax.rsqrt(jnp.sum(k_u * k_u, axis=-1, keepdims=True) + 1e-6)
        kb = k_u * beta
        qs.append(q_u)
        ks.append(k_u)
        kbs.append(kb)
        rhs.append(jnp.concatenate([v_u * beta, kb * eg], axis=1))
        egs.append(eg)
        gcols.append(gcol)
    s_old = [s_ref[j, h] for j, h in units]
    amats = [_mm(jnp.concatenate([kb, q_u], axis=0), k_u, NT, PASSES_MAIN)
             for kb, q_u, k_u in zip(kbs, qs, ks)]
    lmats = [jnp.where(strict, am[:tb] * dc, 0.0) for am, dc in zip(amats, decays)]
    attns = [am[tb:] * dc for am, dc in zip(amats, decays)]
    tinvs = _inv_unit_lower([-lm for lm in lmats], levels, G_PASSES_INV)
    uws = [_mm(t, rr, NN, PASSES_MAIN) for t, rr in zip(tinvs, rhs)]
    xss = [_mm(jnp.concatenate([uw[:, G_HEAD:], q_u * eg], axis=0), s, NN, PASSES_MAIN)
           for uw, q_u, eg, s in zip(uws, qs, egs, s_old)]
    vnews = [uw[:, :G_HEAD] - xs[:tb] for uw, xs in zip(uws, xss)]
    os_ = [xs[tb:] + _mm(at, vn, NN, PASSES_MAIN) for xs, at, vn in zip(xss, attns, vnews)]
    ktails = [k_u * jnp.exp(gc[tb - 1:tb, :] - gc) for k_u, gc in zip(ks, gcols)]
    upd = [_mm(kt, vn, TN, PASSES_MAIN) for kt, vn in zip(ktails, vnews)]

    out_rows = []
    for j in range(sb):
        outs = []
        for h in range(nh):
            i = j * nh + h
            s_ref[j, h] = s_old[i] * egs[i][tb - 1:tb, :] + upd[i]
            o = os_[i]
            o = o * lax.rsqrt(jnp.mean(o * o, axis=-1, keepdims=True) + NORM_EPS) * nw
            outs.append(o)
        out_rows.append(jnp.concatenate(outs, axis=1))
    out = out_rows[0] if sb == 1 else jnp.concatenate(out_rows, axis=0)
    yb_ref[...] = (out * _silu(z)).reshape(sb, tb, HBW).astype(yb_ref.dtype)

    @pl.when(ci == pl.num_programs(2) - 1)
    def _():
        sout_ref[...] = s_ref[...]


def _gdn(p_all, nb, t_pad, t_valid, gw, goff, baoff, prm, state):
    tb = min(CHUNK, t_pad)
    nc = t_pad // tb
    sb = _seq_block(nb, t_pad, nc, G_LONG_SEQ_BLOCK)
    nhb = gw // HBW
    heads = gw // G_HEAD
    qb = goff // HBW
    kb, vb, zb = qb + gw // HBW, qb + 2 * gw // HBW, qb + 3 * gw // HBW
    bab = baoff // LANES
    has_state = state is not None
    levels = max(1, (min(tb, t_valid or tb) - 1).bit_length())
    yb_dtype = BF16 if tb % (2 * SUBLANES) == 0 else F32

    def pcol(off):
        return pl.BlockSpec((sb, tb, HBW), lambda b, h, ci: (b, ci, off + h))

    def cwspec(off):
        return pl.BlockSpec((G_CONV, HBW), lambda b, h, ci: (0, off + h))

    in_specs = [pcol(qb), pcol(kb), pcol(vb), pcol(zb),
                pl.BlockSpec((sb, tb, LANES), lambda b, h, ci: (b, ci, bab + h)),
                cwspec(0), cwspec(gw // HBW), cwspec(2 * gw // HBW),
                pl.BlockSpec((1, LANES), lambda b, h, ci: (0, h)),
                pl.BlockSpec((1, LANES), lambda b, h, ci: (0, h)),
                pl.BlockSpec((1, G_HEAD), lambda b, h, ci: (0, 0))]
    args = [p_all, p_all, p_all, p_all, p_all, prm["conv_w"], prm["conv_w"], prm["conv_w"],
            prm["neg_a"], prm["dt"], prm["norm_w"]]
    state_spec = pl.BlockSpec((sb, G_HB, G_HEAD, G_HEAD), lambda b, h, ci: (b, h, 0, 0))
    if has_state:
        s0, hist = state
        in_specs += [state_spec,
                     pl.BlockSpec((sb, SUBLANES, HBW), lambda b, h, ci: (b, 0, h)),
                     pl.BlockSpec((sb, SUBLANES, HBW), lambda b, h, ci: (b, 0, gw // HBW + h)),
                     pl.BlockSpec((sb, SUBLANES, HBW), lambda b, h, ci: (b, 0, 2 * gw // HBW + h))]
        args += [s0, hist, hist, hist]
    kern = functools.partial(_gdn_kernel, tb=tb, sb=sb, t_valid=t_valid, has_state=has_state,
                             levels=levels)
    return pl.pallas_call(
        kern,
        grid=(nb // sb, nhb, nc),
        in_specs=in_specs,
        out_specs=[pl.BlockSpec((sb, tb, HBW), lambda b, h, ci: (b, ci, h)), state_spec],
        out_shape=[jax.ShapeDtypeStruct((nb, t_pad, gw), yb_dtype),
                   jax.ShapeDtypeStruct((nb, heads, G_HEAD, G_HEAD), F32)],
        scratch_shapes=[pltpu.VMEM((sb, G_HB, G_HEAD, G_HEAD), F32),
                        pltpu.VMEM((3, sb, SUBLANES, HBW), F32)],
        compiler_params=_cparams(("parallel", "parallel", "arbitrary")),
        name="gdn_chunk",
    )(*args)


def _run_group(x, mods, nb, t, lay, wts, rprm, gprm, states):
    m, d = x.shape
    rw, gw = lay["rw"], lay["gw"]
    tm = _pick(m, (512, 256, 128, 64, 32, 16, 8))
    if mods.per_seq:
        tm = _pick(t, (512, 256, 128, 64, 32, 16, 8))
    n_all = wts["w_all"].shape[1]
    tm_norm = min(tm, 256)

    h = _norm_mod(x, wts["norm1_w"], mods, 0, 1, tm_norm)
    p_all = _matmul(h, wts["w_all"], tm, _pick(n_all, (1024, 512, 256, 128)), F32, "in_proj")

    p3 = p_all.reshape(nb, t, n_all)
    shift_new = p3[:, t - 1, :lay["r_cols"]]
    conv_new = p3[:, t - (G_CONV - 1):, lay["goff"]:lay["goff"] + 3 * gw]

    if t % SUBLANES == 0:
        t_pad, t_valid, p_rec = t, None, p3
    else:
        t_pad = -(-t // SUBLANES) * SUBLANES
        t_valid = t
        p_rec = jnp.pad(p3, ((0, 0), (0, t_pad - t), (0, 0)))

    ya, wkv_new = _rwkv(p_rec, nb, t_pad, t_valid, rw, rprm, states and states["rwkv"])
    yb, gdn_new = _gdn(p_rec, nb, t_pad, t_valid, gw, lay["goff"], lay["baoff"], gprm,
                       states and states["gdn"])
    ya = ya[:, :t].reshape(m, rw).astype(BF16)
    yb = yb[:, :t].reshape(m, gw).astype(BF16)

    tn_m = _pick(d, (1024, 512, 256, 128))
    while lay["moff"] % tn_m:
        tn_m //= 2
    merged = _merge(ya, yb, wts["w_out_a"], wts["w_out_b"], p_all, lay["moff"], tm, tn_m)
    x1 = _out_resid(merged, wts["w_out"], x, mods, 2, tm, _pick(d, (1024, 512, 256, 128)))
    h2 = _norm_mod(x1, wts["norm2_w"], mods, 3, 4, tm_norm)
    dff = wts["w_up"].shape[1]
    up = _up_proj(h2, wts["w_up"], tm, _pick(dff, (1024, 512, 256, 128)))
    y = _down_final(up, wts["w_down"], x1, mods, 5, wts["final_norm_w"], tm,
                    _pick(dff, (1024, 512, 256, 128)))
    return y, wkv_new, shift_new, gdn_new, conv_new


def kernel(x_prompt, x_sample, state_rwkv_wkv, state_rwkv_shift, state_gdn, state_gdn_conv,
           c_prompt, c_sample, norm1_w, norm2_w, w_ada, b_ada, w_in, r_mu, r_w0, r_w_w2, r_a0,
           r_w_a2, r_w_g2, r_k_k, r_k_a, r_r_k, r_lnx_w, r_lnx_b, g_conv_w, g_a_log, g_dt_bias,
           g_norm_w, w_out_a, w_out_b, w_out, w_up, w_down, final_norm_w):
    depth = w_in.shape[0]
    assert depth == 1, "single-layer trunk"
    bp, tp, d = x_prompt.shape
    bs, ts, _ = x_sample.shape
    rw = r_w0.shape[1]
    gw = g_conv_w.shape[2] // 3
    g_heads = g_a_log.shape[1]
    dlw, dla, dlg = r_w_w2.shape[1], r_w_a2.shape[1], r_w_g2.shape[1]
    lora = dlw + dla + dlg
    r_cols = 3 * rw + lora
    g_cols = 4 * gw + 2 * g_heads
    assert rw % HBW == 0 and gw % HBW == 0 and lora <= HBW and g_heads % G_HB == 0
    assert w_in.shape[2] == r_cols + g_cols + 2 * d

    rseg = 3 * rw + HBW
    goff = rseg
    baoff = goff + 4 * gw
    n_ba = (g_heads // G_HB) * LANES
    moff = baoff + n_ba
    lay = dict(rw=rw, gw=gw, r_cols=r_cols, goff=goff, baoff=baoff, moff=moff)

    wi = w_in[0]
    w_b = wi[:, r_cols + 4 * gw:r_cols + 4 * gw + g_heads].reshape(d, g_heads // G_HB, G_HB)
    w_a = wi[:, r_cols + 4 * gw + g_heads:r_cols + g_cols].reshape(d, g_heads // G_HB, G_HB)
    w_ba = jnp.concatenate([w_b, w_a, jnp.zeros((d, g_heads // G_HB, LANES - 2 * G_HB), F32)],
                           axis=2).reshape(d, n_ba)
    w_all = jnp.concatenate([wi[:, :r_cols], jnp.zeros((d, rseg - r_cols), F32),
                             wi[:, r_cols:r_cols + 4 * gw], w_ba,
                             wi[:, r_cols + g_cols:]], axis=1).astype(BF16)

    wts = dict(w_all=w_all, norm1_w=norm1_w[0], norm2_w=norm2_w[0],
               w_out_a=w_out_a[0].astype(BF16), w_out_b=w_out_b[0].astype(BF16),
               w_out=w_out[0].astype(BF16), w_up=w_up[0].astype(BF16),
               w_down=w_down[0].astype(BF16), final_norm_w=final_norm_w)

    row = lambda a: a.reshape(1, -1)

    def hilo(w):
        hi = w.astype(BF16)
        return jnp.concatenate([hi, (w - hi.astype(F32)).astype(BF16)], axis=0)

    rprm = dict(mu=jnp.pad(row(r_mu[0]), ((0, 0), (0, rseg - r_cols))),
                w0=row(r_w0[0]), a0=row(r_a0[0]), k_k=row(r_k_k[0]), k_a=row(r_k_a[0]),
                r_k=row(r_r_k[0]), lnx_w=row(r_lnx_w[0]), lnx_b=row(r_lnx_b[0]),
                ww2=hilo(r_w_w2[0]), wa2=hilo(r_w_a2[0]), wg2=hilo(r_w_g2[0]))

    def ba_row(vals):
        v = vals.reshape(g_heads // G_HB, G_HB)
        z = jnp.zeros_like(v)
        return jnp.concatenate([z, v, jnp.zeros((g_heads // G_HB, LANES - 2 * G_HB), F32)],
                               axis=1).reshape(1, n_ba)

    gprm = dict(conv_w=g_conv_w[0], neg_a=ba_row(-jnp.exp(g_a_log[0])), dt=ba_row(g_dt_bias[0]),
                norm_w=row(g_norm_w[0]))

    c_all = jnp.concatenate([c_prompt, c_sample], axis=0)
    n_seq = bp + bs
    c_all = jnp.pad(c_all, ((0, -n_seq % SUBLANES), (0, 0)))
    mods_all = _adaln(c_all, w_ada[0], b_ada[0])
    mods_p = _Mods(mods_all[:bp].reshape(bp, 1, 6 * d), True, d, tp)
    mods_s = _Mods(jnp.repeat(mods_all[bp:n_seq], ts, axis=0), False, d, ts)

    y_p, p_wkv, p_shift, p_gdn, p_conv = _run_group(
        x_prompt.reshape(bp * tp, d), mods_p, bp, tp, lay, wts, rprm, gprm, None)

    shift0 = jnp.pad(state_rwkv_shift[0], ((0, 0), (0, rseg - r_cols))).reshape(bs, 1, rseg)
    hist0 = jnp.pad(state_gdn_conv[0], ((0, 0), (SUBLANES - (G_CONV - 1), 0), (0, 0)))
    states = dict(rwkv=(state_rwkv_wkv[0], shift0), gdn=(state_gdn[0], hist0))
    y_s, s_wkv, s_shift, s_gdn, s_conv = _run_group(
        x_sample.reshape(bs * ts, d), mods_s, bs, ts, lay, wts, rprm, gprm, states)

    return (y_p.reshape(bp, tp, d), y_s.reshape(bs, ts, d),
            p_wkv[None], p_shift[None], p_gdn[None], p_conv[None],
            s_wkv[None], s_shift[None], s_gdn[None], s_conv[None])
```

```python
import functools

import jax
import jax.numpy as jnp
from jax import lax
from jax.experimental import pallas as pl
from jax.experimental.pallas import tpu as pltpu

F32 = jnp.float32
BF16 = jnp.bfloat16

LANES = 128
SUBLANES = 8
VMEM_LIMIT = 48 * 1024 * 1024
VMEM_LIMIT_WIDE = 56 * 1024 * 1024

R_HEAD = 64
G_HEAD = 128
G_CONV = 4
R_HB = 8
G_HB = 4
HBW = 512
CHUNK = 64
SEQ_BLOCK_ROWS = 64
R_LONG_SEQ_BLOCK = 4
G_LONG_SEQ_BLOCK = 4
NORM_EPS = 1e-6
RWKV_GN_EPS = 64e-5
PASSES_MAIN = 1
R_PASSES_INV = 3
G_PASSES_INV = 3

NN = ((1,), (0,))
NT = ((1,), (1,))
TN = ((0,), (0,))


def _cparams(sem, vmem=VMEM_LIMIT):
    return pltpu.CompilerParams(dimension_semantics=sem, vmem_limit_bytes=vmem)


def _pick(n, cands):
    for c in cands:
        if n % c == 0:
            return c
    raise ValueError(f"no tile in {cands} divides {n}")


def _dot(a, b, dims=NN):
    return lax.dot_general(a, b, (dims, ((), ())), preferred_element_type=F32)


def _mm1(a, b, dims=NN):
    return _dot(a.astype(BF16), b.astype(BF16), dims)


def _split2(a):
    hi = a.astype(BF16)
    lo = (a - hi.astype(F32)).astype(BF16)
    return hi, lo


def _mm3(a, b, dims=NN):
    ah, al = _split2(a)
    bh, bl = _split2(b)
    if dims == TN:
        return _dot(ah, bh, dims) + (_dot(ah, bl, dims) + _dot(al, bh, dims))
    m = a.shape[0]
    ah32 = ah.astype(F32)
    stacked = jnp.concatenate([ah32, a - ah32], axis=0).astype(BF16)
    r = _dot(stacked, bh, dims)
    return r[:m] + (_dot(ah, bl, dims) + r[m:])


def _mm(a, b, dims=NN, passes=1):
    return _mm1(a, b, dims) if passes == 1 else _mm3(a, b, dims)


def _mm3_presplit(a, b_hl):
    m = a.shape[0]
    kdim = b_hl.shape[0] // 2
    ah = a.astype(BF16)
    ah32 = ah.astype(F32)
    stacked = jnp.concatenate([ah32, a - ah32], axis=0).astype(BF16)
    r = _dot(stacked, b_hl[:kdim])
    return r[:m] + (_dot(ah, b_hl[kdim:]) + r[m:])


def _mm_exact_lhs(a_bf16, b, dims=NN):
    b1 = b.astype(BF16)
    r1 = b - b1.astype(F32)
    b2 = r1.astype(BF16)
    b3 = (r1 - b2.astype(F32)).astype(BF16)
    return _dot(a_bf16, b1, dims) + (_dot(a_bf16, b2, dims) + _dot(a_bf16, b3, dims))


def _sigmoid(x):
    return 1.0 / (1.0 + jnp.exp(-x))


def _silu(x):
    return x * _sigmoid(x)


def _softplus(x):
    return jnp.maximum(x, 0.0) + jnp.log1p(jnp.exp(-jnp.abs(x)))


def _tri_masks(c, reps=1):
    row = lax.broadcasted_iota(jnp.int32, (c, reps * c), 0)
    col = lax.broadcasted_iota(jnp.int32, (c, reps * c), 1) % c
    return row > col, row >= col


def _block_cumsum_matrix(rows, tb):
    row = lax.broadcasted_iota(jnp.int32, (rows, rows), 0)
    col = lax.broadcasted_iota(jnp.int32, (rows, rows), 1)
    keep = (row >= col) & (row // tb == col // tb)
    return jnp.where(keep, 1.0, 0.0).astype(BF16)


INV_PACK = 4


def _inv_unit_lower(ns, levels, passes):
    c = ns[0].shape[0]
    pack = INV_PACK if (c * INV_PACK == 2 * LANES and len(ns) % INV_PACK == 0) else 1
    width = pack * c
    row = lax.broadcasted_iota(jnp.int32, (c, width), 0)
    col = lax.broadcasted_iota(jnp.int32, (c, width), 1)
    eye = jnp.where(row == col % c, 1.0, 0.0).astype(F32)
    blk_of_lane = lax.broadcasted_iota(jnp.int32, (1, width), 1) // c

    def bdiag(w):
        if pack == 1:
            return w
        zero = jnp.zeros_like(w)
        return jnp.concatenate([jnp.where(blk_of_lane == i, w, zero) for i in range(pack)], axis=0)

    def mul(lhs, p):
        if passes == 1:
            return _dot(lhs.astype(BF16), bdiag(p.astype(BF16)))
        m = lhs.shape[0]
        ph = p.astype(BF16)
        pl_ = (p - ph.astype(F32)).astype(BF16)
        lh = lhs.astype(BF16)
        lh32 = lh.astype(F32)
        stacked = jnp.concatenate([lh32, lhs - lh32], axis=0).astype(BF16)
        r = _dot(stacked, bdiag(ph))
        return r[:m] + (_dot(lh, bdiag(pl_)) + r[m:])

    packed = [ns[q] if pack == 1 else jnp.concatenate(ns[q:q + pack], axis=1)
              for q in range(0, len(ns), pack)]
    ts = [eye + n for n in packed]
    if levels > 1:
        ps = [mul(n, n) for n in packed]
        for lvl in range(1, levels):
            if lvl == levels - 1:
                ts = [t + mul(t, p) for t, p in zip(ts, ps)]
            else:
                rs = [mul(jnp.concatenate([t, p], axis=0), p) for t, p in zip(ts, ps)]
                ts = [t + r[:c] for t, r in zip(ts, rs)]
                ps = [r[c:] for r in rs]
    if pack == 1:
        return ts
    return [t[:, i * c:(i + 1) * c] for t in ts for i in range(pack)]


def _shift_rows(x, hists, k, tb):
    rows = x.shape[0]
    xr = pltpu.roll(x, k, 0)
    hr = [pltpu.roll(h, k, 0) for h in hists]
    if tb == SUBLANES:
        hfull = hr[0] if len(hr) == 1 else jnp.concatenate(hr, axis=0)
        row = lax.broadcasted_iota(jnp.int32, (rows, 1), 0)
        return jnp.where(row % tb < k, hfull, xr)
    row8 = lax.broadcasted_iota(jnp.int32, (SUBLANES, 1), 0)
    pieces = []
    for j, h in enumerate(hr):
        pieces.append(jnp.where(row8 < k, h, xr[j * tb:j * tb + SUBLANES]))
        pieces.append(xr[j * tb + SUBLANES:(j + 1) * tb])
    return jnp.concatenate(pieces, axis=0)


def _rows_from(parts, tb):
    tiles = [jnp.broadcast_to(p, (tb, p.shape[1])) for p in parts]
    return tiles[0] if len(tiles) == 1 else jnp.concatenate(tiles, axis=0)


def _adaln_kernel(c_ref, w_ref, b_ref, o_ref):
    c = c_ref[...]
    s = _silu(c)
    o_ref[...] = _mm1(s, w_ref[...]) + b_ref[...]


def _adaln(c_all, w_ada, b_ada):
    m, d = c_all.shape
    n = w_ada.shape[1]
    tn = _pick(n, (512, 256, 128))
    return pl.pallas_call(
        _adaln_kernel,
        grid=(n // tn,),
        in_specs=[pl.BlockSpec((m, d), lambda j: (0, 0)),
                  pl.BlockSpec((d, tn), lambda j: (0, j)),
                  pl.BlockSpec((1, tn), lambda j: (0, j))],
        out_specs=pl.BlockSpec((m, tn), lambda j: (0, j)),
        out_shape=jax.ShapeDtypeStruct((m, n), F32),
        compiler_params=_cparams(("parallel",)),
        name="adaln",
    )(c_all, w_ada, b_ada.reshape(1, n))


class _Mods:
    def __init__(self, arr, per_seq, d, rows_per_seq):
        self.arr, self.per_seq, self.d, self.rows_per_seq = arr, per_seq, d, rows_per_seq

    def spec(self, tm, tn, which, ij):
        off = which * self.d // tn
        if self.per_seq:
            tps = self.rows_per_seq // tm
            return pl.BlockSpec((1, 1, tn), lambda *g: (ij(*g)[0] // tps, 0, off + ij(*g)[1]))
        return pl.BlockSpec((tm, tn), lambda *g: (ij(*g)[0], off + ij(*g)[1]))


def _mod_val(ref):
    v = ref[...]
    return v[0] if v.ndim == 3 else v


def _norm_mod_kernel(x_ref, w_ref, sc_ref, sh_ref, o_ref):
    x = x_ref[...]
    y = x * lax.rsqrt(jnp.mean(x * x, axis=-1, keepdims=True) + NORM_EPS) * w_ref[...]
    o_ref[...] = (y * (1.0 + _mod_val(sc_ref)) + _mod_val(sh_ref)).astype(o_ref.dtype)


def _norm_mod(x, w, mods, sh_idx, sc_idx, tm):
    m, d = x.shape
    ij = lambda i: (i, 0)
    return pl.pallas_call(
        _norm_mod_kernel,
        grid=(m // tm,),
        in_specs=[pl.BlockSpec((tm, d), lambda i: (i, 0)),
                  pl.BlockSpec((1, d), lambda i: (0, 0)),
                  mods.spec(tm, d, sc_idx, ij),
                  mods.spec(tm, d, sh_idx, ij)],
        out_specs=pl.BlockSpec((tm, d), lambda i: (i, 0)),
        out_shape=jax.ShapeDtypeStruct((m, d), BF16),
        compiler_params=_cparams(("parallel",)),
        name="norm_mod",
    )(x, w.reshape(1, d), mods.arr, mods.arr)


def _mm_kernel(a_ref, w_ref, o_ref):
    o_ref[...] = _dot(a_ref[...], w_ref[...]).astype(o_ref.dtype)


def _matmul(a, w, tm, tn, out_dtype, name):
    m, k = a.shape
    n = w.shape[1]
    return pl.pallas_call(
        _mm_kernel,
        grid=(n // tn, m // tm),
        in_specs=[pl.BlockSpec((tm, k), lambda j, i: (i, 0)),
                  pl.BlockSpec((k, tn), lambda j, i: (0, j))],
        out_specs=pl.BlockSpec((tm, tn), lambda j, i: (i, j)),
        out_shape=jax.ShapeDtypeStruct((m, n), out_dtype),
        compiler_params=_cparams(("parallel", "parallel")),
        name=name,
    )(a, w)


def _relu2_kernel(a_ref, w_ref, o_ref):
    u = jnp.maximum(_dot(a_ref[...], w_ref[...]), 0.0)
    o_ref[...] = (u * u).astype(o_ref.dtype)


def _up_proj(a, w, tm, tn):
    m, k = a.shape
    n = w.shape[1]
    return pl.pallas_call(
        _relu2_kernel,
        grid=(n // tn, m // tm),
        in_specs=[pl.BlockSpec((tm, k), lambda j, i: (i, 0)),
                  pl.BlockSpec((k, tn), lambda j, i: (0, j))],
        out_specs=pl.BlockSpec((tm, tn), lambda j, i: (i, j)),
        out_shape=jax.ShapeDtypeStruct((m, n), BF16),
        compiler_params=_cparams(("parallel", "parallel")),
        name="up_proj",
    )(a, w)


def _merge_kernel(ya_ref, yb_ref, wa_ref, wb_ref, ga_ref, gb_ref, o_ref):
    pa = _dot(ya_ref[...], wa_ref[...])
    pb = _dot(yb_ref[...], wb_ref[...])
    o_ref[...] = (_sigmoid(ga_ref[...]) * pa + _sigmoid(gb_ref[...]) * pb).astype(o_ref.dtype)


def _merge(ya, yb, wa, wb, p_all, moff, tm, tn):
    m, ka = ya.shape
    kb = yb.shape[1]
    d = wa.shape[1]
    oa, ob = moff // tn, (moff + d) // tn
    return pl.pallas_call(
        _merge_kernel,
        grid=(d // tn, m // tm),
        in_specs=[pl.BlockSpec((tm, ka), lambda j, i: (i, 0)),
                  pl.BlockSpec((tm, kb), lambda j, i: (i, 0)),
                  pl.BlockSpec((ka, tn), lambda j, i: (0, j)),
                  pl.BlockSpec((kb, tn), lambda j, i: (0, j)),
                  pl.BlockSpec((tm, tn), lambda j, i: (i, oa + j)),
                  pl.BlockSpec((tm, tn), lambda j, i: (i, ob + j))],
        out_specs=pl.BlockSpec((tm, tn), lambda j, i: (i, j)),
        out_shape=jax.ShapeDtypeStruct((m, d), BF16),
        compiler_params=_cparams(("parallel", "parallel")),
        name="merge",
    )(ya, yb, wa, wb, p_all, p_all)


def _resid_kernel(a_ref, w_ref, x_ref, g_ref, o_ref):
    o_ref[...] = x_ref[...] + _mod_val(g_ref) * _dot(a_ref[...], w_ref[...])


def _out_resid(a, w, x, mods, gate_idx, tm, tn):
    m, k = a.shape
    n = w.shape[1]
    ij = lambda j, i: (i, j)
    return pl.pallas_call(
        _resid_kernel,
        grid=(n // tn, m // tm),
        in_specs=[pl.BlockSpec((tm, k), lambda j, i: (i, 0)),
                  pl.BlockSpec((k, tn), lambda j, i: (0, j)),
                  pl.BlockSpec((tm, tn), lambda j, i: (i, j)),
                  mods.spec(tm, tn, gate_idx, ij)],
        out_specs=pl.BlockSpec((tm, tn), lambda j, i: (i, j)),
        out_shape=jax.ShapeDtypeStruct((m, n), F32),
        compiler_params=_cparams(("parallel", "parallel")),
        name="out_resid",
    )(a, w, x, mods.arr)


def _down_kernel(a_ref, w_ref, x_ref, g_ref, fw_ref, o_ref):
    kk = pl.program_id(1)

    @pl.when(kk == 0)
    def _():
        o_ref[...] = jnp.zeros_like(o_ref)

    o_ref[...] = _dot(a_ref[...], w_ref[...]) + o_ref[...]

    @pl.when(kk == pl.num_programs(1) - 1)
    def _():
        x2 = x_ref[...] + _mod_val(g_ref) * o_ref[...]
        y = x2 * lax.rsqrt(jnp.mean(x2 * x2, axis=-1, keepdims=True) + NORM_EPS)
        o_ref[...] = y * fw_ref[...]


def _down_final(a, w, x, mods, gate_idx, final_w, tm, tk):
    m, k = a.shape
    n = w.shape[1]
    ij = lambda i, kk: (i, 0)
    return pl.pallas_call(
        _down_kernel,
        grid=(m // tm, k // tk),
        in_specs=[pl.BlockSpec((tm, tk), lambda i, kk: (i, kk)),
                  pl.BlockSpec((tk, n), lambda i, kk: (kk, 0)),
                  pl.BlockSpec((tm, n), lambda i, kk: (i, 0), pipeline_mode=pl.Buffered(1)),
                  mods.spec(tm, n, gate_idx, ij),
                  pl.BlockSpec((1, n), lambda i, kk: (0, 0))],
        out_specs=pl.BlockSpec((tm, n), lambda i, kk: (i, 0), pipeline_mode=pl.Buffered(1)),
        out_shape=jax.ShapeDtypeStruct((m, n), F32),
        compiler_params=_cparams(("parallel", "arbitrary"), VMEM_LIMIT_WIDE),
        name="down_final",
    )(a, w, x, mods.arr, final_w.reshape(1, n))


def _rwkv_kernel(*refs, tb, sb, t_valid, has_state, levels):
    (r_ref, k_ref, v_ref, lo_ref, mur_ref, muk_ref, muv_ref, mul_ref,
     w0_ref, a0_ref, kk_ref, ka_ref, rk_ref, lw_ref, lb_ref,
     ww2_ref, wa2_ref, wg2_ref) = refs[:18]
    n = 18
    if has_state:
        s0_ref, shr_ref, shk_ref, shv_ref, shl_ref = refs[n:n + 5]
        n += 5
    ya_ref, sout_ref, s_ref, prev_ref = refs[n:n + 4]
    ci = pl.program_id(2)
    rows = sb * tb
    nh = HBW // R_HEAD

    @pl.when(ci == 0)
    def _():
        if has_state:
            s_ref[...] = s0_ref[...]
            for slot, sh in enumerate((shr_ref, shk_ref, shv_ref, shl_ref)):
                for j in range(sb):
                    prev_ref[slot, j, 0:1, :] = sh[j]
        else:
            s_ref[...] = jnp.zeros_like(s_ref)
            prev_ref[...] = jnp.zeros_like(prev_ref)

    row = lax.broadcasted_iota(jnp.int32, (rows, 1), 0)
    is_first = row % tb == 0

    def tshift(x_ref, slot, mu_ref):
        x = x_ref[...].reshape(rows, HBW)
        first = _rows_from([prev_ref[slot, j, 0:1, :] for j in range(sb)], tb)
        prev = jnp.where(is_first, first, pltpu.roll(x, 1, 0))
        for j in range(sb):
            prev_ref[slot, j, 0:1, :] = x[(j + 1) * tb - 1:(j + 1) * tb, :]
        return x + (prev - x) * mu_ref[...]

    r = tshift(r_ref, 0, mur_ref)
    k = tshift(k_ref, 1, muk_ref)
    v = tshift(v_ref, 2, muv_ref)
    lo = tshift(lo_ref, 3, mul_ref)
    dlw = ww2_ref.shape[0] // 2
    dla = wa2_ref.shape[0] // 2
    dlg = wg2_ref.shape[0] // 2
    dw = lo[:, 0:dlw]
    da = lo[:, dlw:dlw + dla]
    dg = lo[:, dlw + dla:dlw + dla + dlg]

    log_w = -_softplus(-(w0_ref[...] + _mm3_presplit(jnp.tanh(dw), ww2_ref[...]))) - 0.5
    lw = -jnp.exp(log_w)
    a = _sigmoid(a0_ref[...] + _mm3_presplit(da, wa2_ref[...]))
    g = _mm3_presplit(_sigmoid(dg), wg2_ref[...])

    if t_valid is not None:
        valid = (row % tb < t_valid).astype(F32)
        lw = lw * valid
        k = k * valid
        v = v * valid

    cum = _mm_exact_lhs(_block_cumsum_matrix(rows, tb), lw)
    w_inc = jnp.exp(cum)
    w_inv = jnp.exp(-cum)
    w_exc = jnp.exp(cum - lw)
    rp = r * w_inc
    kx = k * kk_ref[...]
    k2 = k * (1.0 + (a - 1.0) * ka_ref[...])
    kp = k2 * w_inv
    rk2 = r * k2 * rk_ref[...]
    aw_inv = a * w_inv

    strict2, causal2 = _tri_masks(tb, 2)
    units =[(j, h) for j in range(sb) for h in range(nh)]

    def blk(x, j, h):
        return x[j * tb:(j + 1) * tb, h * R_HEAD:(h + 1) * R_HEAD]

    xs_, ys_, vs_ = [], [], []
    for j, h in units:
        kx_u = blk(kx, j, h)
        kk_u = kx_u * lax.rsqrt(jnp.sum(kx_u * kx_u, axis=-1, keepdims=True) + 1e-6)
        ap = -kk_u * blk(w_exc, j, h)
        bp = kk_u * blk(aw_inv, j, h)
        xs_.append(jnp.concatenate([ap, blk(rp, j, h)], axis=0))
        ys_.append(jnp.concatenate([bp, blk(kp, j, h)], axis=0))
        vs_.append(blk(v, j, h))
    s_old = [s_ref[j, h] for j, h in units]
    amats = [_mm(x, y, NT, PASSES_MAIN) for x, y in zip(xs_, ys_)]
    xss = [_mm(x, s, NT, PASSES_MAIN) for x, s in zip(xs_, s_old)]
    tops = [jnp.where(strict2, am[:tb], 0.0) for am in amats]
    bots = [jnp.where(causal2, am[tb:], 0.0) for am in amats]
    tinvs = _inv_unit_lower([tp[:, :tb] for tp in tops], levels, R_PASSES_INV)
    rhs = [xs[:tb] + _mm(tp[:, tb:], vv, NN, PASSES_MAIN) for xs, tp, vv in zip(xss, tops, vs_)]
    us = [_mm(t, rr, NN, PASSES_MAIN) for t, rr in zip(tinvs, rhs)]
    uvs = [jnp.concatenate([u, vv], axis=0) for u, vv in zip(us, vs_)]
    os_ = [xs[tb:] + _mm(bt, uv, NN, PASSES_MAIN) for xs, bt, uv in zip(xss, bots, uvs)]
    upd = [_mm(uv, y, TN, PASSES_MAIN) for uv, y in zip(uvs, ys_)]
    for i, (j, h) in enumerate(units):
        s_ref[j, h] = (s_old[i] + upd[i]) * blk(w_inc, j, h)[tb - 1:tb, :]

    lnw = lw_ref[...]
    lnb = lb_ref[...]
    out_rows = []
    for j in range(sb):
        outs = []
        for h in range(nh):
            o = os_[j * nh + h]
            mu = jnp.mean(o, axis=-1, keepdims=True)
            var = jnp.mean(jnp.square(o - mu), axis=-1, keepdims=True)
            sl = slice(h * R_HEAD, (h + 1) * R_HEAD)
            on = (o - mu) * lax.rsqrt(var + RWKV_GN_EPS) * lnw[:, sl] + lnb[:, sl]
            bonus = jnp.sum(blk(rk2, j, h), axis=-1, keepdims=True) * vs_[j * nh + h]
            outs.append(on + bonus)
        out_rows.append(jnp.concatenate(outs, axis=1))
    out = out_rows[0] if sb == 1 else jnp.concatenate(out_rows, axis=0)
    ya_ref[...] = (out * g).reshape(sb, tb, HBW).astype(ya_ref.dtype)

    @pl.when(ci == pl.num_programs(2) - 1)
    def _():
        sout_ref[...] = s_ref[...]


def _seq_block(nb, t_pad, nc, long_seq_block):
    sb = long_seq_block if nc > 1 else max(1, SEQ_BLOCK_ROWS // t_pad)
    while nb % sb:
        sb -= 1
    return sb


def _rwkv(p_all, nb, t_pad, t_valid, rw, prm, state):
    tb = min(CHUNK, t_pad)
    nc = t_pad // tb
    sb = _seq_block(nb, t_pad, nc, R_LONG_SEQ_BLOCK)
    nhb = rw // HBW
    heads = rw // R_HEAD
    kb, vb, lb = rw // HBW, 2 * rw // HBW, 3 * rw // HBW
    has_state = state is not None
    levels = max(1, (min(tb, t_valid or tb) - 1).bit_length())
    ya_dtype = BF16 if tb % (2 * SUBLANES) == 0 else F32

    def pcol(off):
        return pl.BlockSpec((sb, tb, HBW), lambda b, h, ci: (b, ci, off + h))

    def vec(off):
        return pl.BlockSpec((1, HBW), lambda b, h, ci: (0, off + h))

    in_specs = [pcol(0), pcol(kb), pcol(vb),
                pl.BlockSpec((sb, tb, HBW), lambda b, h, ci: (b, ci, lb)),
                vec(0), vec(kb), vec(vb),
                pl.BlockSpec((1, HBW), lambda b, h, ci: (0, lb)),
                vec(0), vec(0), vec(0), vec(0), vec(0), vec(0), vec(0),
                pl.BlockSpec((prm["ww2"].shape[0], HBW), lambda b, h, ci: (0, h)),
                pl.BlockSpec((prm["wa2"].shape[0], HBW), lambda b, h, ci: (0, h)),
                pl.BlockSpec((prm["wg2"].shape[0], HBW), lambda b, h, ci: (0, h))]
    args = [p_all, p_all, p_all, p_all, prm["mu"], prm["mu"], prm["mu"], prm["mu"],
            prm["w0"], prm["a0"], prm["k_k"], prm["k_a"], prm["r_k"], prm["lnx_w"], prm["lnx_b"],
            prm["ww2"], prm["wa2"], prm["wg2"]]
    state_spec = pl.BlockSpec((sb, R_HB, R_HEAD, R_HEAD), lambda b, h, ci: (b, h, 0, 0))
    if has_state:
        s0, shift = state
        in_specs += [state_spec,
                     pl.BlockSpec((sb, 1, HBW), lambda b, h, ci: (b, 0, h)),
                     pl.BlockSpec((sb, 1, HBW), lambda b, h, ci: (b, 0, kb + h)),
                     pl.BlockSpec((sb, 1, HBW), lambda b, h, ci: (b, 0, vb + h)),
                     pl.BlockSpec((sb, 1, HBW), lambda b, h, ci: (b, 0, lb))]
        args += [s0, shift, shift, shift, shift]
    kern = functools.partial(_rwkv_kernel, tb=tb, sb=sb, t_valid=t_valid, has_state=has_state,
                             levels=levels)
    return pl.pallas_call(
        kern,
        grid=(nb // sb, nhb, nc),
        in_specs=in_specs,
        out_specs=[pl.BlockSpec((sb, tb, HBW), lambda b, h, ci: (b, ci, h)), state_spec],
        out_shape=[jax.ShapeDtypeStruct((nb, t_pad, rw), ya_dtype),
                   jax.ShapeDtypeStruct((nb, heads, R_HEAD, R_HEAD), F32)],
        scratch_shapes=[pltpu.VMEM((sb, R_HB, R_HEAD, R_HEAD), F32),
                        pltpu.VMEM((4, sb, SUBLANES, HBW), F32)],
        compiler_params=_cparams(("parallel", "parallel", "arbitrary")),
        name="rwkv7_chunk",
    )(*args)


def _gdn_kernel(*refs, tb, sb, t_valid, has_state, levels):
    (q_ref, k_ref, v_ref, z_ref, ba_ref, cq_ref, ck_ref, cv_ref,
     an_ref, dt_ref, nw_ref) = refs[:11]
    n = 11
    if has_state:
        s0_ref, hq_ref, hk_ref, hv_ref = refs[n:n + 4]
        n += 4
    yb_ref, sout_ref, s_ref, hist_ref = refs[n:n + 4]
    ci = pl.program_id(2)
    rows = sb * tb

    @pl.when(ci == 0)
    def _():
        if has_state:
            s_ref[...] = s0_ref[...]
            hist_ref[0] = hq_ref[...]
            hist_ref[1] = hk_ref[...]
            hist_ref[2] = hv_ref[...]
        else:
            s_ref[...] = jnp.zeros_like(s_ref)
            hist_ref[...] = jnp.zeros_like(hist_ref)

    row = lax.broadcasted_iota(jnp.int32, (rows, 1), 0)

    def conv(x_ref, slot, cw_ref):
        x = x_ref[...].reshape(rows, HBW)
        hists =[hist_ref[slot, j] for j in range(sb)]
        cw = cw_ref[...]
        acc = _shift_rows(x, hists, 3, tb) * cw[0:1, :]
        acc = acc + _shift_rows(x, hists, 2, tb) * cw[1:2, :]
        acc = acc + _shift_rows(x, hists, 1, tb) * cw[2:3, :]
        acc = acc + x * cw[3:4, :]
        for j in range(sb):
            hist_ref[slot, j] = x[(j + 1) * tb - SUBLANES:(j + 1) * tb, :]
        return _silu(acc)

    q = conv(q_ref, 0, cq_ref)
    k = conv(k_ref, 1, ck_ref)
    v = conv(v_ref, 2, cv_ref)
    z = z_ref[...].reshape(rows, HBW)
    ba = ba_ref[...].reshape(rows, LANES)
    beta_all = _sigmoid(ba)
    la_all = an_ref[...] * _softplus(ba + dt_ref[...])
    if t_valid is not None:
        valid = (row % tb < t_valid).astype(F32)
        beta_all = beta_all * valid
        la_all = la_all * valid
        q = q * valid
        k = k * valid
        v = v * valid

    g_all = _mm_exact_lhs(_block_cumsum_matrix(rows, tb), la_all)
    lrow = lax.broadcasted_iota(jnp.int32, (LANES, LANES), 0)
    lcol = lax.broadcasted_iota(jnp.int32, (LANES, LANES), 1)
    eye = jnp.where(lrow == lcol, 1.0, 0.0).astype(BF16)
    g_t = _mm_exact_lhs(eye, g_all, NT)
    eg_all = jnp.exp(g_all)
    strict, causal = _tri_masks(tb)
    nw = nw_ref[...]
    nh = q.shape[1] // G_HEAD
    units = [(j, h) for j in range(sb) for h in range(nh)]

    def blk(x, j, h):
        return x[j * tb:(j + 1) * tb, h * G_HEAD:(h + 1) * G_HEAD]

    qs, ks, kbs, rhs, decays, egs, gcols = [], [], [], [], [], [], []
    for j, h in units:
        rs = slice(j * tb, (j + 1) * tb)
        gcol = g_all[rs, G_HB + h:G_HB + h + 1]
        grow = g_t[G_HB + h:G_HB + h + 1, rs]
        beta = beta_all[rs, h:h + 1]
        eg = eg_all[rs, G_HB + h:G_HB + h + 1]
        decays.append(jnp.where(causal, jnp.exp(jnp.where(causal, gcol - grow, 0.0)), 0.0))
        q_u, k_u, v_u = blk(q, j, h), blk(k, j, h), blk(v, j, h)
        q_u = q_u * lax.rsqrt(jnp.sum(q_u * q_u, axis=-1, keepdims=True) + 1e-6) * (G_HEAD ** -0.5)
        k_u = k_u * lax.rsqrt(jnp.sum(k_u * k_u, axis=-1, keepdims=True) + 1e-6)
        kb = k_u * beta
        qs.append(q_u)
        ks.append(k_u)
        kbs.append(kb)
        rhs.append(jnp.concatenate([v_u * beta, kb * eg], axis=1))
        egs.append(eg)
        gcols.append(gcol)
    s_old = [s_ref[j, h] for j, h in units]
    amats = [_mm(jnp.concatenate([kb, q_u], axis=0), k_u, NT, PASSES_MAIN)
             for kb, q_u, k_u in zip(kbs, qs, ks)]
    lmats = [jnp.where(strict, am[:tb] * dc, 0.0) for am, dc in zip(amats, decays)]
    attns = [am[tb:] * dc for am, dc in zip(amats, decays)]
    tinvs = _inv_unit_lower([-lm for lm in lmats], levels, G_PASSES_INV)
    uws = [_mm(t, rr, NN, PASSES_MAIN) for t, rr in zip(tinvs, rhs)]
    xss = [_mm(jnp.concatenate([uw[:, G_HEAD:], q_u * eg], axis=0), s, NN, PASSES_MAIN)
           for uw, q_u, eg, s in zip(uws, qs, egs, s_old)]
    vnews = [uw[:, :G_HEAD] - xs[:tb] for uw, xs in zip(uws, xss)]
    os_ = [xs[tb:] + _mm(at, vn, NN, PASSES_MAIN) for xs, at, vn in zip(xss, attns, vnews)]
    ktails = [k_u * jnp.exp(gc[tb - 1:tb, :] - gc) for k_u, gc in zip(ks, gcols)]
    upd = [_mm(kt, vn, TN, PASSES_MAIN) for kt, vn in zip(ktails, vnews)]

    out_rows = []
    for j in range(sb):
        outs = []
        for h in range(nh):
            i = j * nh + h
            s_ref[j, h] = s_old[i] * egs[i][tb - 1:tb, :] + upd[i]
            o = os_[i]
            o = o * lax.rsqrt(jnp.mean(o * o, axis=-1, keepdims=True) + NORM_EPS) * nw
            outs.append(o)
        out_rows.append(jnp.concatenate(outs, axis=1))
    out = out_rows[0] if sb == 1 else jnp.concatenate(out_rows, axis=0)
    yb_ref[...] = (out * _silu(z)).reshape(sb, tb, HBW).astype(yb_ref.dtype)

    @pl.when(ci == pl.num_programs(2) - 1)
    def _():
        sout_ref[...] = s_ref[...]


def _gdn(p_all, nb, t_pad, t_valid, gw, goff, baoff, prm, state):
    tb = min(CHUNK, t_pad)
    nc = t_pad // tb
    sb = _seq_block(nb, t_pad, nc, G_LONG_SEQ_BLOCK)
    nhb = gw // HBW
    heads = gw // G_HEAD
    qb = goff // HBW
    kb, vb, zb = qb + gw // HBW, qb + 2 * gw // HBW, qb + 3 * gw // HBW
    bab = baoff // LANES
    has_state = state is not None
    levels = max(1, (min(tb, t_valid or tb) - 1).bit_length())
    yb_dtype = BF16 if tb % (2 * SUBLANES) == 0 else F32

    def pcol(off):
        return pl.BlockSpec((sb, tb, HBW), lambda b, h, ci: (b, ci, off + h))

    def cwspec(off):
        return pl.BlockSpec((G_CONV, HBW), lambda b, h, ci: (0, off + h))

    in_specs = [pcol(qb), pcol(kb), pcol(vb), pcol(zb),
                pl.BlockSpec((sb, tb, LANES), lambda b, h, ci: (b, ci, bab + h)),
                cwspec(0), cwspec(gw // HBW), cwspec(2 * gw // HBW),
                pl.BlockSpec((1, LANES), lambda b, h, ci: (0, h)),
                pl.BlockSpec((1, LANES), lambda b, h, ci: (0, h)),
                pl.BlockSpec((1, G_HEAD), lambda b, h, ci: (0, 0))]
    args = [p_all, p_all, p_all, p_all, p_all, prm["conv_w"], prm["conv_w"], prm["conv_w"],
            prm["neg_a"], prm["dt"], prm["norm_w"]]
    state_spec = pl.BlockSpec((sb, G_HB, G_HEAD, G_HEAD), lambda b, h, ci: (b, h, 0, 0))
    if has_state:
        s0, hist = state
        in_specs += [state_spec,
                     pl.BlockSpec((sb, SUBLANES, HBW), lambda b, h, ci: (b, 0, h)),
                     pl.BlockSpec((sb, SUBLANES, HBW), lambda b, h, ci: (b, 0, gw // HBW + h)),
                     pl.BlockSpec((sb, SUBLANES, HBW), lambda b, h, ci: (b, 0, 2 * gw // HBW + h))]
        args += [s0, hist, hist, hist]
    kern = functools.partial(_gdn_kernel, tb=tb, sb=sb, t_valid=t_valid, has_state=has_state,
                             levels=levels)
    return pl.pallas_call(
        kern,
        grid=(nb // sb, nhb, nc),
        in_specs=in_specs,
        out_specs=[pl.BlockSpec((sb, tb, HBW), lambda b, h, ci: (b, ci, h)), state_spec],
        out_shape=[jax.ShapeDtypeStruct((nb, t_pad, gw), yb_dtype),
                   jax.ShapeDtypeStruct((nb, heads, G_HEAD, G_HEAD), F32)],
        scratch_shapes=[pltpu.VMEM((sb, G_HB, G_HEAD, G_HEAD), F32),
                        pltpu.VMEM((3, sb, SUBLANES, HBW), F32)],
        compiler_params=_cparams(("parallel", "parallel", "arbitrary")),
        name="gdn_chunk",
    )(*args)


def _run_group(x, mods, nb, t, lay, wts, rprm, gprm, states):
    m, d = x.shape
    rw, gw = lay["rw"], lay["gw"]
    tm = _pick(m, (512, 256, 128, 64, 32, 16, 8))
    if mods.per_seq:
        tm = _pick(t, (512, 256, 128, 64, 32, 16, 8))
    n_all = wts["w_all"].shape[1]
    tm_norm = min(tm, 256)

    h = _norm_mod(x, wts["norm1_w"], mods, 0, 1, tm_norm)
    p_all = _matmul(h, wts["w_all"], tm, _pick(n_all, (1024, 512, 256, 128)), F32, "in_proj")

    p3 = p_all.reshape(nb, t, n_all)
    shift_new = p3[:, t - 1, :lay["r_cols"]]
    conv_new = p3[:, t - (G_CONV - 1):, lay["goff"]:lay["goff"] + 3 * gw]

    if t % SUBLANES == 0:
        t_pad, t_valid, p_rec = t, None, p3
    else:
        t_pad = -(-t // SUBLANES) * SUBLANES
        t_valid = t
        p_rec = jnp.pad(p3, ((0, 0), (0, t_pad - t), (0, 0)))

    ya, wkv_new = _rwkv(p_rec, nb, t_pad, t_valid, rw, rprm, states and states["rwkv"])
    yb, gdn_new = _gdn(p_rec, nb, t_pad, t_valid, gw, lay["goff"], lay["baoff"], gprm,
                       states and states["gdn"])
    ya = ya[:, :t].reshape(m, rw).astype(BF16)
    yb = yb[:, :t].reshape(m, gw).astype(BF16)

    tn_m = _pick(d, (1024, 512, 256, 128))
    while lay["moff"] % tn_m:
        tn_m //= 2
    merged = _merge(ya, yb, wts["w_out_a"], wts["w_out_b"], p_all, lay["moff"], tm, tn_m)
    x1 = _out_resid(merged, wts["w_out"], x, mods, 2, tm, _pick(d, (1024, 512, 256, 128)))
    h2 = _norm_mod(x1, wts["norm2_w"], mods, 3, 4, tm_norm)
    dff = wts["w_up"].shape[1]
    up = _up_proj(h2, wts["w_up"], tm, _pick(dff, (1024, 512, 256, 128)))
    y = _down_final(up, wts["w_down"], x1, mods, 5, wts["final_norm_w"], tm,
                    _pick(dff, (1024, 512, 256, 128)))
    return y, wkv_new, shift_new, gdn_new, conv_new


def kernel(x_prompt, x_sample, state_rwkv_wkv, state_rwkv_shift, state_gdn, state_gdn_conv,
           c_prompt, c_sample, norm1_w, norm2_w, w_ada, b_ada, w_in, r_mu, r_w0, r_w_w2, r_a0,
           r_w_a2, r_w_g2, r_k_k, r_k_a, r_r_k, r_lnx_w, r_lnx_b, g_conv_w, g_a_log, g_dt_bias,
           g_norm_w, w_out_a, w_out_b, w_out, w_up, w_down, final_norm_w):
    depth = w_in.shape[0]
    assert depth == 1, "single-layer trunk"
    bp, tp, d = x_prompt.shape
    bs, ts, _ = x_sample.shape
    rw = r_w0.shape[1]
    gw = g_conv_w.shape[2] // 3
    g_heads = g_a_log.shape[1]
    dlw, dla, dlg = r_w_w2.shape[1], r_w_a2.shape[1], r_w_g2.shape[1]
    lora = dlw + dla + dlg
    r_cols = 3 * rw + lora
    g_cols = 4 * gw + 2 * g_heads
    assert rw % HBW == 0 and gw % HBW == 0 and lora <= HBW and g_heads % G_HB == 0
    assert w_in.shape[2] == r_cols + g_cols + 2 * d

    rseg = 3 * rw + HBW
    goff = rseg
    baoff = goff + 4 * gw
    n_ba = (g_heads // G_HB) * LANES
    moff = baoff + n_ba
    lay = dict(rw=rw, gw=gw, r_cols=r_cols, goff=goff, baoff=baoff, moff=moff)

    wi = w_in[0]
    w_b = wi[:, r_cols + 4 * gw:r_cols + 4 * gw + g_heads].reshape(d, g_heads // G_HB, G_HB)
    w_a = wi[:, r_cols + 4 * gw + g_heads:r_cols + g_cols].reshape(d, g_heads // G_HB, G_HB)
    w_ba = jnp.concatenate([w_b, w_a, jnp.zeros((d, g_heads // G_HB, LANES - 2 * G_HB), F32)],
                           axis=2).reshape(d, n_ba)
    w_all = jnp.concatenate([wi[:, :r_cols], jnp.zeros((d, rseg - r_cols), F32),
                             wi[:, r_cols:r_cols + 4 * gw], w_ba,
                             wi[:, r_cols + g_cols:]], axis=1).astype(BF16)

    wts = dict(w_all=w_all, norm1_w=norm1_w[0], norm2_w=norm2_w[0],
               w_out_a=w_out_a[0].astype(BF16), w_out_b=w_out_b[0].astype(BF16),
               w_out=w_out[0].astype(BF16), w_up=w_up[0].astype(BF16),
               w_down=w_down[0].astype(BF16), final_norm_w=final_norm_w)

    row = lambda a: a.reshape(1, -1)

    def hilo(w):
        hi = w.astype(BF16)
        return jnp.concatenate([hi, (w - hi.astype(F32)).astype(BF16)], axis=0)

    rprm = dict(mu=jnp.pad(row(r_mu[0]), ((0, 0), (0, rseg - r_cols))),
                w0=row(r_w0[0]), a0=row(r_a0[0]), k_k=row(r_k_k[0]), k_a=row(r_k_a[0]),
                r_k=row(r_r_k[0]), lnx_w=row(r_lnx_w[0]), lnx_b=row(r_lnx_b[0]),
                ww2=hilo(r_w_w2[0]), wa2=hilo(r_w_a2[0]), wg2=hilo(r_w_g2[0]))

    def ba_row(vals):
        v = vals.reshape(g_heads // G_HB, G_HB)
        z = jnp.zeros_like(v)
        return jnp.concatenate([z, v, jnp.zeros((g_heads // G_HB, LANES - 2 * G_HB), F32)],
                               axis=1).reshape(1, n_ba)

    gprm = dict(conv_w=g_conv_w[0], neg_a=ba_row(-jnp.exp(g_a_log[0])), dt=ba_row(g_dt_bias[0]),
                norm_w=row(g_norm_w[0]))

    c_all = jnp.concatenate([c_prompt, c_sample], axis=0)
    n_seq = bp + bs
    c_all = jnp.pad(c_all, ((0, -n_seq % SUBLANES), (0, 0)))
    mods_all = _adaln(c_all, w_ada[0], b_ada[0])
    mods_p = _Mods(mods_all[:bp].reshape(bp, 1, 6 * d), True, d, tp)
    mods_s = _Mods(jnp.repeat(mods_all[bp:n_seq], ts, axis=0), False, d, ts)

    y_p, p_wkv, p_shift, p_gdn, p_conv = _run_group(
        x_prompt.reshape(bp * tp, d), mods_p, bp, tp, lay, wts, rprm, gprm, None)

    shift0 = jnp.pad(state_rwkv_shift[0], ((0, 0), (0, rseg - r_cols))).reshape(bs, 1, rseg)
    hist0 = jnp.pad(state_gdn_conv[0], ((0, 0), (SUBLANES - (G_CONV - 1), 0), (0, 0)))
    states = dict(rwkv=(state_rwkv_wkv[0], shift0), gdn=(state_gdn[0], hist0))
    y_s, s_wkv, s_shift, s_gdn, s_conv = _run_group(
        x_sample.reshape(bs * ts, d), mods_s, bs, ts, lay, wts, rprm, gprm, states)

    return (y_p.reshape(bp, tp, d), y_s.reshape(bs, ts, d),
            p_wkv[None], p_shift[None], p_gdn[None], p_conv[None],
            s_wkv[None], s_shift[None], s_gdn[None], s_conv[None])
```

```python
import functools

import jax
import jax.numpy as jnp
from jax import lax
from jax.experimental import pallas as pl
from jax.experimental.pallas import tpu as pltpu

F32 = jnp.float32
BF16 = jnp.bfloat16

LANES = 128
SUBLANES = 8
VMEM_LIMIT = 48 * 1024 * 1024
VMEM_LIMIT_WIDE = 56 * 1024 * 1024

R_HEAD = 64
G_HEAD = 128
G_CONV = 4
R_HB = 8
G_HB = 4
HBW = 512
CHUNK = 64
PROJ_TN = 512
PROJ_ROWS = 512
SEQ_BLOCK_ROWS = 64
R_LONG_SEQ_BLOCK = 4
G_LONG_SEQ_BLOCK = 4
NORM_EPS = 1e-6
RWKV_GN_EPS = 64e-5
PASSES_MAIN = 1
R_PASSES_INV = 3
G_PASSES_INV = 3

NN = ((1,), (0,))
NT = ((1,), (1,))
TN = ((0,), (0,))


def _cparams(sem, vmem=VMEM_LIMIT):
    return pltpu.CompilerParams(dimension_semantics=sem, vmem_limit_bytes=vmem)


def _pick(n, cands):
    for c in cands:
        if n % c == 0:
            return c
    raise ValueError(f"no tile in {cands} divides {n}")


def _dot(a, b, dims=NN):
    return lax.dot_general(a, b, (dims, ((), ())), preferred_element_type=F32)


def _mm1(a, b, dims=NN):
    return _dot(a.astype(BF16), b.astype(BF16), dims)


def _split2(a):
    hi = a.astype(BF16)
    lo = (a - hi.astype(F32)).astype(BF16)
    return hi, lo


def _mm3(a, b, dims=NN):
    ah, al = _split2(a)
    bh, bl = _split2(b)
    if dims == TN:
        return _dot(ah, bh, dims) + (_dot(ah, bl, dims) + _dot(al, bh, dims))
    m = a.shape[0]
    ah32 = ah.astype(F32)
    stacked = jnp.concatenate([ah32, a - ah32], axis=0).astype(BF16)
    r = _dot(stacked, bh, dims)
    return r[:m] + (_dot(ah, bl, dims) + r[m:])


def _mm(a, b, dims=NN, passes=1):
    return _mm1(a, b, dims) if passes == 1 else _mm3(a, b, dims)


def _mm3_presplit(a, b_hl):
    m = a.shape[0]
    kdim = b_hl.shape[0] // 2
    ah = a.astype(BF16)
    ah32 = ah.astype(F32)
    stacked = jnp.concatenate([ah32, a - ah32], axis=0).astype(BF16)
    r = _dot(stacked, b_hl[:kdim])
    return r[:m] + (_dot(ah, b_hl[kdim:]) + r[m:])


def _mm_exact_lhs(a_bf16, b, dims=NN):
    b1 = b.astype(BF16)
    r1 = b - b1.astype(F32)
    b2 = r1.astype(BF16)
    b3 = (r1 - b2.astype(F32)).astype(BF16)
    return _dot(a_bf16, b1, dims) + (_dot(a_bf16, b2, dims) + _dot(a_bf16, b3, dims))


def _sigmoid(x):
    return 1.0 / (1.0 + jnp.exp(-x))


def _silu(x):
    return x * _sigmoid(x)


def _softplus(x):
    return jnp.maximum(x, 0.0) + jnp.log1p(jnp.exp(-jnp.abs(x)))


def _tri_masks(c, reps=1):
    row = lax.broadcasted_iota(jnp.int32, (c, reps * c), 0)
    col = lax.broadcasted_iota(jnp.int32, (c, reps * c), 1) % c
    return row > col, row >= col


def _block_cumsum_matrix(rows, tb):
    row = lax.broadcasted_iota(jnp.int32, (rows, rows), 0)
    col = lax.broadcasted_iota(jnp.int32, (rows, rows), 1)
    keep = (row >= col) & (row // tb == col // tb)
    return jnp.where(keep, 1.0, 0.0).astype(BF16)


INV_PACK = 4


def _inv_unit_lower(ns, levels, passes):
    c = ns[0].shape[0]
    pack = INV_PACK if (c * INV_PACK == 2 * LANES and len(ns) % INV_PACK == 0) else 1
    width = pack * c
    row = lax.broadcasted_iota(jnp.int32, (c, width), 0)
    col = lax.broadcasted_iota(jnp.int32, (c, width), 1)
    eye = jnp.where(row == col % c, 1.0, 0.0).astype(F32)
    blk_of_lane = lax.broadcasted_iota(jnp.int32, (1, width), 1) // c

    def bdiag(w):
        if pack == 1:
            return w
        zero = jnp.zeros_like(w)
        return jnp.concatenate([jnp.where(blk_of_lane == i, w, zero) for i in range(pack)], axis=0)

    def mul(lhs, p):
        if passes == 1:
            return _dot(lhs.astype(BF16), bdiag(p.astype(BF16)))
        m = lhs.shape[0]
        ph = p.astype(BF16)
        pl_ = (p - ph.astype(F32)).astype(BF16)
        lh = lhs.astype(BF16)
        lh32 = lh.astype(F32)
        stacked = jnp.concatenate([lh32, lhs - lh32], axis=0).astype(BF16)
        r = _dot(stacked, bdiag(ph))
        return r[:m] + (_dot(lh, bdiag(pl_)) + r[m:])

    packed = [ns[q] if pack == 1 else jnp.concatenate(ns[q:q + pack], axis=1)
              for q in range(0, len(ns), pack)]
    ts = [eye + n for n in packed]
    if levels > 1:
        ps = [mul(n, n) for n in packed]
        for lvl in range(1, levels):
            if lvl == levels - 1:
                ts = [t + mul(t, p) for t, p in zip(ts, ps)]
            else:
                rs = [mul(jnp.concatenate([t, p], axis=0), p) for t, p in zip(ts, ps)]
                ts = [t + r[:c] for t, r in zip(ts, rs)]
                ps = [r[c:] for r in rs]
    if pack == 1:
        return ts
    return [t[:, i * c:(i + 1) * c] for t in ts for i in range(pack)]


def _shift_rows(x, hists, k, tb):
    rows = x.shape[0]
    xr = pltpu.roll(x, k, 0)
    hr = [pltpu.roll(h, k, 0) for h in hists]
    if tb == SUBLANES:
        hfull = hr[0] if len(hr) == 1 else jnp.concatenate(hr, axis=0)
        row = lax.broadcasted_iota(jnp.int32, (rows, 1), 0)
        return jnp.where(row % tb < k, hfull, xr)
    row8 = lax.broadcasted_iota(jnp.int32, (SUBLANES, 1), 0)
    pieces = []
    for j, h in enumerate(hr):
        pieces.append(jnp.where(row8 < k, h, xr[j * tb:j * tb + SUBLANES]))
        pieces.append(xr[j * tb + SUBLANES:(j + 1) * tb])
    return jnp.concatenate(pieces, axis=0)


def _rows_from(parts, tb):
    tiles = [jnp.broadcast_to(p, (tb, p.shape[1])) for p in parts]
    return tiles[0] if len(tiles) == 1 else jnp.concatenate(tiles, axis=0)


def _adaln_kernel(c_ref, w_ref, b_ref, o_ref):
    c = c_ref[...]
    s = _silu(c)
    o_ref[...] = _mm1(s, w_ref[...]) + b_ref[...]


def _adaln(c_all, w_ada, b_ada):
    m, d = c_all.shape
    n = w_ada.shape[1]
    tn = _pick(n, (512, 256, 128))
    return pl.pallas_call(
        _adaln_kernel,
        grid=(n // tn,),
        in_specs=[pl.BlockSpec((m, d), lambda j: (0, 0)),
                  pl.BlockSpec((d, tn), lambda j: (0, j)),
                  pl.BlockSpec((1, tn), lambda j: (0, j))],
        out_specs=pl.BlockSpec((m, tn), lambda j: (0, j)),
        out_shape=jax.ShapeDtypeStruct((m, n), F32),
        compiler_params=_cparams(("parallel",)),
        name="adaln",
    )(c_all, w_ada, b_ada.reshape(1, n))


class _Mods:
    def __init__(self, arr, per_seq, d, rows_per_seq):
        self.arr, self.per_seq, self.d, self.rows_per_seq = arr, per_seq, d, rows_per_seq

    def spec(self, tm, tn, which, ij):
        off = which * self.d // tn
        if self.per_seq:
            tps = self.rows_per_seq // tm
            return pl.BlockSpec((1, 1, tn), lambda *g: (ij(*g)[0] // tps, 0, off + ij(*g)[1]))
        return pl.BlockSpec((tm, tn), lambda *g: (ij(*g)[0], off + ij(*g)[1]))


def _mod_val(ref):
    v = ref[...]
    return v[0] if v.ndim == 3 else v


def _norm_mod_kernel(x_ref, w_ref, sc_ref, sh_ref, o_ref):
    x = x_ref[...]
    y = x * lax.rsqrt(jnp.mean(x * x, axis=-1, keepdims=True) + NORM_EPS) * w_ref[...]
    o_ref[...] = (y * (1.0 + _mod_val(sc_ref)) + _mod_val(sh_ref)).astype(o_ref.dtype)


def _norm_mod(x, w, mods, sh_idx, sc_idx, tm):
    m, d = x.shape
    ij = lambda i: (i, 0)
    return pl.pallas_call(
        _norm_mod_kernel,
        grid=(m // tm,),
        in_specs=[pl.BlockSpec((tm, d), lambda i: (i, 0)),
                  pl.BlockSpec((1, d), lambda i: (0, 0)),
                  mods.spec(tm, d, sc_idx, ij),
                  mods.spec(tm, d, sh_idx, ij)],
        out_specs=pl.BlockSpec((tm, d), lambda i: (i, 0)),
        out_shape=jax.ShapeDtypeStruct((m, d), BF16),
        compiler_params=_cparams(("parallel",)),
        name="norm_mod",
    )(x, w.reshape(1, d), mods.arr, mods.arr)


def _mm_kernel(a_ref, w_ref, o_ref):
    o_ref[...] = _dot(a_ref[...], w_ref[...]).astype(o_ref.dtype)


def _matmul(a, w, tm, tn, out_dtype, name):
    m, k = a.shape
    n = w.shape[1]
    return pl.pallas_call(
        _mm_kernel,
        grid=(n // tn, m // tm),
        in_specs=[pl.BlockSpec((tm, k), lambda j, i: (i, 0)),
                  pl.BlockSpec((k, tn), lambda j, i: (0, j))],
        out_specs=pl.BlockSpec((tm, tn), lambda j, i: (i, j)),
        out_shape=jax.ShapeDtypeStruct((m, n), out_dtype),
        compiler_params=_cparams(("parallel", "parallel")),
        name=name,
    )(a, w)


def _cast_weight_tile(w_refs, off, wb_ref):
    k, tn = wb_ref.shape
    for r0 in range(0, k, PROJ_ROWS):
        w = jnp.concatenate([r[r0:r0 + PROJ_ROWS, :] for r in w_refs], axis=1)
        wb_ref[r0:r0 + PROJ_ROWS, :] = w[:, off:off + tn].astype(BF16)


def _w_blocks(w, col0, row_of):
    k = w.shape[0]
    base, off = divmod(col0, LANES)
    nblk = PROJ_TN // LANES + (1 if off else 0)
    specs = [pl.BlockSpec((k, LANES), lambda *g, q=q: (0, base + (PROJ_TN // LANES) * row_of(*g) + q))
             for q in range(nblk)]
    return specs, off


def _proj_kernel(*refs, nblk, off, epilogue):
    a_ref = refs[0]
    w_refs = refs[1:1 + nblk]
    extra = refs[1 + nblk:-2]
    o_ref, wb_ref = refs[-2:]

    @pl.when(pl.program_id(1) == 0)
    def _():
        _cast_weight_tile(w_refs, off, wb_ref)

    o_ref[...] = epilogue(_dot(a_ref[...], wb_ref[...]), *extra).astype(o_ref.dtype)


def _proj(a, w, col0, n_out, tm, out_dtype, name, epilogue=lambda acc: acc, extra=(), extra_specs=()):
    m, k = a.shape
    w_specs, off = _w_blocks(w, col0, lambda j, i: j)
    kern = functools.partial(_proj_kernel, nblk=len(w_specs), off=off, epilogue=epilogue)
    return pl.pallas_call(
        kern,
        grid=(n_out // PROJ_TN, m // tm),
        in_specs=[pl.BlockSpec((tm, k), lambda j, i: (i, 0))] + w_specs + list(extra_specs),
        out_specs=pl.BlockSpec((tm, PROJ_TN), lambda j, i: (i, j)),
        out_shape=jax.ShapeDtypeStruct((m, n_out), out_dtype),
        scratch_shapes=[pltpu.VMEM((k, PROJ_TN), BF16)],
        compiler_params=_cparams(("parallel", "arbitrary")),
        name=name,
    )(a, *([w] * len(w_specs)), *extra)


def _relu2(acc):
    u = jnp.maximum(acc, 0.0)
    return u * u


def _resid(acc, x_ref, g_ref):
    return x_ref[...] + _mod_val(g_ref) * acc


def _out_resid(a, w, x, mods, gate_idx, tm):
    ij = lambda j, i: (i, j)
    return _proj(a, w, 0, w.shape[1], tm, F32, "out_resid", _resid, (x, mods.arr),
                 (pl.BlockSpec((tm, PROJ_TN), lambda j, i: (i, j)), mods.spec(tm, PROJ_TN, gate_idx, ij)))


def _merge_kernel(*refs, nblk):
    ya_ref, yb_ref = refs[:2]
    wa_refs = refs[2:2 + nblk]
    wb_refs = refs[2 + nblk:2 + 2 * nblk]
    ga_ref, gb_ref, o_ref, wa_s, wb_s = refs[2 + 2 * nblk:]

    @pl.when(pl.program_id(1) == 0)
    def _():
        _cast_weight_tile(wa_refs, 0, wa_s)
        _cast_weight_tile(wb_refs, 0, wb_s)

    pa = _dot(ya_ref[...], wa_s[...])
    pb = _dot(yb_ref[...], wb_s[...])
    o_ref[...] = (_sigmoid(ga_ref[...]) * pa + _sigmoid(gb_ref[...]) * pb).astype(o_ref.dtype)


def _merge(ya, yb, wa, wb, p_m, tm):
    m, ka = ya.shape
    kb = yb.shape[1]
    d = wa.shape[1]
    wa_specs, _ = _w_blocks(wa, 0, lambda j, i: j)
    wb_specs, _ = _w_blocks(wb, 0, lambda j, i: j)
    nblk = len(wa_specs)
    return pl.pallas_call(
        functools.partial(_merge_kernel, nblk=nblk),
        grid=(d // PROJ_TN, m // tm),
        in_specs=[pl.BlockSpec((tm, ka), lambda j, i: (i, 0)),
                  pl.BlockSpec((tm, kb), lambda j, i: (i, 0))] + wa_specs + wb_specs + [
                  pl.BlockSpec((tm, PROJ_TN), lambda j, i: (i, j)),
                  pl.BlockSpec((tm, PROJ_TN), lambda j, i: (i, d // PROJ_TN + j))],
        out_specs=pl.BlockSpec((tm, PROJ_TN), lambda j, i: (i, j)),
        out_shape=jax.ShapeDtypeStruct((m, d), BF16),
        scratch_shapes=[pltpu.VMEM((ka, PROJ_TN), BF16), pltpu.VMEM((kb, PROJ_TN), BF16)],
        compiler_params=_cparams(("parallel", "arbitrary")),
        name="merge",
    )(ya, yb, *([wa] * nblk), *([wb] * nblk), p_m, p_m)


def _down_kernel(a_ref, w_ref, x_ref, g_ref, fw_ref, o_ref):
    kk = pl.program_id(1)

    @pl.when(kk == 0)
    def _():
        o_ref[...] = jnp.zeros_like(o_ref)

    o_ref[...] = _dot(a_ref[...], w_ref[...]) + o_ref[...]

    @pl.when(kk == pl.num_programs(1) - 1)
    def _():
        x2 = x_ref[...] + _mod_val(g_ref) * o_ref[...]
        y = x2 * lax.rsqrt(jnp.mean(x2 * x2, axis=-1, keepdims=True) + NORM_EPS)
        o_ref[...] = y * fw_ref[...]


def _down_final(a, w, x, mods, gate_idx, final_w, tm, tk):
    m, k = a.shape
    n = w.shape[1]
    ij = lambda i, kk: (i, 0)
    return pl.pallas_call(
        _down_kernel,
        grid=(m // tm, k // tk),
        in_specs=[pl.BlockSpec((tm, tk), lambda i, kk: (i, kk)),
                  pl.BlockSpec((tk, n), lambda i, kk: (kk, 0)),
                  pl.BlockSpec((tm, n), lambda i, kk: (i, 0), pipeline_mode=pl.Buffered(1)),
                  mods.spec(tm, n, gate_idx, ij),
                  pl.BlockSpec((1, n), lambda i, kk: (0, 0))],
        out_specs=pl.BlockSpec((tm, n), lambda i, kk: (i, 0), pipeline_mode=pl.Buffered(1)),
        out_shape=jax.ShapeDtypeStruct((m, n), F32),
        compiler_params=_cparams(("parallel", "arbitrary"), VMEM_LIMIT_WIDE),
        name="down_final",
    )(a, w, x, mods.arr, final_w.reshape(1, n))


def _rwkv_kernel(*refs, tb, sb, t_valid, has_state, levels):
    (r_ref, k_ref, v_ref, lo_ref, mur_ref, muk_ref, muv_ref, mul_ref,
     w0_ref, a0_ref, kk_ref, ka_ref, rk_ref, lw_ref, lb_ref,
     ww2_ref, wa2_ref, wg2_ref) = refs[:18]
    n = 18
    if has_state:
        s0_ref, shr_ref, shk_ref, shv_ref, shl_ref = refs[n:n + 5]
        n += 5
    ya_ref, sout_ref, s_ref, prev_ref = refs[n:n + 4]
    ci = pl.program_id(2)
    rows = sb * tb
    nh = HBW // R_HEAD

    @pl.when(ci == 0)
    def _():
        if has_state:
            s_ref[...] = s0_ref[...]
            for slot, sh in enumerate((shr_ref, shk_ref, shv_ref, shl_ref)):
                for j in range(sb):
                    prev_ref[slot, j, 0:1, :] = sh[j]
        else:
            s_ref[...] = jnp.zeros_like(s_ref)
            prev_ref[...] = jnp.zeros_like(prev_ref)

    row = lax.broadcasted_iota(jnp.int32, (rows, 1), 0)
    is_first = row % tb == 0

    def tshift(x_ref, slot, mu_ref):
        x = x_ref[...].reshape(rows, HBW)
        first = _rows_from([prev_ref[slot, j, 0:1, :] for j in range(sb)], tb)
        prev = jnp.where(is_first, first, pltpu.roll(x, 1, 0))
        for j in range(sb):
            prev_ref[slot, j, 0:1, :] = x[(j + 1) * tb - 1:(j + 1) * tb, :]
        return x + (prev - x) * mu_ref[...]

    r = tshift(r_ref, 0, mur_ref)
    k = tshift(k_ref, 1, muk_ref)
    v = tshift(v_ref, 2, muv_ref)
    lo = tshift(lo_ref, 3, mul_ref)
    dlw = ww2_ref.shape[0] // 2
    dla = wa2_ref.shape[0] // 2
    dlg = wg2_ref.shape[0] // 2
    dw = lo[:, 0:dlw]
    da = lo[:, dlw:dlw + dla]
    dg = lo[:, dlw + dla:dlw + dla + dlg]

    log_w = -_softplus(-(w0_ref[...] + _mm3_presplit(jnp.tanh(dw), ww2_ref[...]))) - 0.5
    lw = -jnp.exp(log_w)
    a = _sigmoid(a0_ref[...] + _mm3_presplit(da, wa2_ref[...]))
    g = _mm3_presplit(_sigmoid(dg), wg2_ref[...])

    if t_valid is not None:
        valid = (row % tb < t_valid).astype(F32)
        lw = lw * valid
        k = k * valid
        v = v * valid

    cum = _mm_exact_lhs(_block_cumsum_matrix(rows, tb), lw)
    w_inc = jnp.exp(cum)
    w_inv = jnp.exp(-cum)
    w_exc = jnp.exp(cum - lw)
    rp = r * w_inc
    kx = k * kk_ref[...]
    k2 = k * (1.0 + (a - 1.0) * ka_ref[...])
    kp = k2 * w_inv
    rk2 = r * k2 * rk_ref[...]
    aw_inv = a * w_inv

    strict2, causal2 = _tri_masks(tb, 2)
    units =[(j, h) for j in range(sb) for h in range(nh)]

    def blk(x, j, h):
        return x[j * tb:(j + 1) * tb, h * R_HEAD:(h + 1) * R_HEAD]

    xs_, ys_, vs_ = [], [], []
    for j, h in units:
        kx_u = blk(kx, j, h)
        kk_u = kx_u * lax.rsqrt(jnp.sum(kx_u * kx_u, axis=-1, keepdims=True) + 1e-6)
        ap = -kk_u * blk(w_exc, j, h)
        bp = kk_u * blk(aw_inv, j, h)
        xs_.append(jnp.concatenate([ap, blk(rp, j, h)], axis=0))
        ys_.append(jnp.concatenate([bp, blk(kp, j, h)], axis=0))
        vs_.append(blk(v, j, h))
    s_old = [s_ref[j, h] for j, h in units]
    amats = [_mm(x, y, NT, PASSES_MAIN) for x, y in zip(xs_, ys_)]
    xss = [_mm(x, s, NT, PASSES_MAIN) for x, s in zip(xs_, s_old)]
    tops = [jnp.where(strict2, am[:tb], 0.0) for am in amats]
    bots = [jnp.where(causal2, am[tb:], 0.0) for am in amats]
    tinvs = _inv_unit_lower([tp[:, :tb] for tp in tops], levels, R_PASSES_INV)
    rhs = [xs[:tb] + _mm(tp[:, tb:], vv, NN, PASSES_MAIN) for xs, tp, vv in zip(xss, tops, vs_)]
    us = [_mm(t, rr, NN, PASSES_MAIN) for t, rr in zip(tinvs, rhs)]
    uvs = [jnp.concatenate([u, vv], axis=0) for u, vv in zip(us, vs_)]
    os_ = [xs[tb:] + _mm(bt, uv, NN, PASSES_MAIN) for xs, bt, uv in zip(xss, bots, uvs)]
    upd = [_mm(uv, y, TN, PASSES_MAIN) for uv, y in zip(uvs, ys_)]
    for i, (j, h) in enumerate(units):
        s_ref[j, h] = (s_old[i] + upd[i]) * blk(w_inc, j, h)[tb - 1:tb, :]

    lnw = lw_ref[...]
    lnb = lb_ref[...]
    out_rows = []
    for j in range(sb):
        outs = []
        for h in range(nh):
            o = os_[j * nh + h]
            mu = jnp.mean(o, axis=-1, keepdims=True)
            var = jnp.mean(jnp.square(o - mu), axis=-1, keepdims=True)
            sl = slice(h * R_HEAD, (h + 1) * R_HEAD)
            on = (o - mu) * lax.rsqrt(var + RWKV_GN_EPS) * lnw[:, sl] + lnb[:, sl]
            bonus = jnp.sum(blk(rk2, j, h), axis=-1, keepdims=True) * vs_[j * nh + h]
            outs.append(on + bonus)
        out_rows.append(jnp.concatenate(outs, axis=1))
    out = out_rows[0] if sb == 1 else jnp.concatenate(out_rows, axis=0)
    ya_ref[...] = (out * g).reshape(sb, tb, HBW).astype(ya_ref.dtype)

    @pl.when(ci == pl.num_programs(2) - 1)
    def _():
        sout_ref[...] = s_ref[...]


def _seq_block(nb, t_pad, nc, long_seq_block):
    sb = long_seq_block if nc > 1 else max(1, SEQ_BLOCK_ROWS // t_pad)
    while nb % sb:
        sb -= 1
    return sb


def _rwkv(p_all, nb, t_pad, t_valid, rw, prm, state):
    tb = min(CHUNK, t_pad)
    nc = t_pad // tb
    sb = _seq_block(nb, t_pad, nc, R_LONG_SEQ_BLOCK)
    nhb = rw // HBW
    heads = rw // R_HEAD
    kb, vb, lb = rw // HBW, 2 * rw // HBW, 3 * rw // HBW
    has_state = state is not None
    levels = max(1, (min(tb, t_valid or tb) - 1).bit_length())
    ya_dtype = BF16 if tb % (2 * SUBLANES) == 0 else F32

    def pcol(off):
        return pl.BlockSpec((sb, tb, HBW), lambda b, h, ci: (b, ci, off + h))

    def vec(off):
        return pl.BlockSpec((1, HBW), lambda b, h, ci: (0, off + h))

    in_specs = [pcol(0), pcol(kb), pcol(vb),
                pl.BlockSpec((sb, tb, HBW), lambda b, h, ci: (b, ci, lb)),
                vec(0), vec(kb), vec(vb),
                pl.BlockSpec((1, HBW), lambda b, h, ci: (0, lb)),
                vec(0), vec(0), vec(0), vec(0), vec(0), vec(0), vec(0),
                pl.BlockSpec((prm["ww2"].shape[0], HBW), lambda b, h, ci: (0, h)),
                pl.BlockSpec((prm["wa2"].shape[0], HBW), lambda b, h, ci: (0, h)),
                pl.BlockSpec((prm["wg2"].shape[0], HBW), lambda b, h, ci: (0, h))]
    args = [p_all, p_all, p_all, p_all, prm["mu"], prm["mu"], prm["mu"], prm["mu"],
            prm["w0"], prm["a0"], prm["k_k"], prm["k_a"], prm["r_k"], prm["lnx_w"], prm["lnx_b"],
            prm["ww2"], prm["wa2"], prm["wg2"]]
    state_spec = pl.BlockSpec((sb, R_HB, R_HEAD, R_HEAD), lambda b, h, ci: (b, h, 0, 0))
    if has_state:
        s0, shift = state
        in_specs += [state_spec,
                     pl.BlockSpec((sb, 1, HBW), lambda b, h, ci: (b, 0, h)),
                     pl.BlockSpec((sb, 1, HBW), lambda b, h, ci: (b, 0, kb + h)),
                     pl.BlockSpec((sb, 1, HBW), lambda b, h, ci: (b, 0, vb + h)),
                     pl.BlockSpec((sb, 1, HBW), lambda b, h, ci: (b, 0, lb))]
        args += [s0, shift, shift, shift, shift]
    kern = functools.partial(_rwkv_kernel, tb=tb, sb=sb, t_valid=t_valid, has_state=has_state,
                             levels=levels)
    return pl.pallas_call(
        kern,
        grid=(nb // sb, nhb, nc),
        in_specs=in_specs,
        out_specs=[pl.BlockSpec((sb, tb, HBW), lambda b, h, ci: (b, ci, h)), state_spec],
        out_shape=[jax.ShapeDtypeStruct((nb, t_pad, rw), ya_dtype),
                   jax.ShapeDtypeStruct((nb, heads, R_HEAD, R_HEAD), F32)],
        scratch_shapes=[pltpu.VMEM((sb, R_HB, R_HEAD, R_HEAD), F32),
                        pltpu.VMEM((4, sb, SUBLANES, HBW), F32)],
        compiler_params=_cparams(("parallel", "parallel", "arbitrary")),
        name="rwkv7_chunk",
    )(*args)


def _gdn_kernel(*refs, tb, sb, t_valid, has_state, levels):
    (q_ref, k_ref, v_ref, z_ref, ba_ref, cq_ref, ck_ref, cv_ref,
     an_ref, dt_ref, nw_ref) = refs[:11]
    n = 11
    if has_state:
        s0_ref, hq_ref, hk_ref, hv_ref = refs[n:n + 4]
        n += 4
    yb_ref, sout_ref, s_ref, hist_ref = refs[n:n + 4]
    ci = pl.program_id(2)
    rows = sb * tb

    @pl.when(ci == 0)
    def _():
        if has_state:
            s_ref[...] = s0_ref[...]
            hist_ref[0] = hq_ref[...]
            hist_ref[1] = hk_ref[...]
            hist_ref[2] = hv_ref[...]
        else:
            s_ref[...] = jnp.zeros_like(s_ref)
            hist_ref[...] = jnp.zeros_like(hist_ref)

    row = lax.broadcasted_iota(jnp.int32, (rows, 1), 0)

    def conv(x_ref, slot, cw_ref):
        x = x_ref[...].reshape(rows, HBW)
        hists =[hist_ref[slot, j] for j in range(sb)]
        cw = cw_ref[...]
        acc = _shift_rows(x, hists, 3, tb) * cw[0:1, :]
        acc = acc + _shift_rows(x, hists, 2, tb) * cw[1:2, :]
        acc = acc + _shift_rows(x, hists, 1, tb) * cw[2:3, :]
        acc = acc + x * cw[3:4, :]
        for j in range(sb):
            hist_ref[slot, j] = x[(j + 1) * tb - SUBLANES:(j + 1) * tb, :]
        return _silu(acc)

    q = conv(q_ref, 0, cq_ref)
    k = conv(k_ref, 1, ck_ref)
    v = conv(v_ref, 2, cv_ref)
    z = z_ref[...].reshape(rows, HBW)
    ba = ba_ref[...].reshape(rows, LANES)
    beta_all = _sigmoid(ba)
    la_all = an_ref[...] * _softplus(ba + dt_ref[...])
    if t_valid is not None:
        valid = (row % tb < t_valid).astype(F32)
        beta_all = beta_all * valid
        la_all = la_all * valid
        q = q * valid
        k = k * valid
        v = v * valid

    g_all = _mm_exact_lhs(_block_cumsum_matrix(rows, tb), la_all)
    lrow = lax.broadcasted_iota(jnp.int32, (LANES, LANES), 0)
    lcol = lax.broadcasted_iota(jnp.int32, (LANES, LANES), 1)
    eye = jnp.where(lrow == lcol, 1.0, 0.0).astype(BF16)
    g_t = _mm_exact_lhs(eye, g_all, NT)
    eg_all = jnp.exp(g_all)
    strict, causal = _tri_masks(tb)
    nw = nw_ref[...]
    nh = q.shape[1] // G_HEAD
    units = [(j, h) for j in range(sb) for h in range(nh)]

    def blk(x, j, h):
        return x[j * tb:(j + 1) * tb, h * G_HEAD:(h + 1) * G_HEAD]

    qs, ks, kbs, rhs, decays, egs, gcols = [], [], [], [], [], [], []
    for j, h in units:
        rs = slice(j * tb, (j + 1) * tb)
        gcol = g_all[rs, G_HB + h:G_HB + h + 1]
        grow = g_t[G_HB + h:G_HB + h + 1, rs]
        beta = beta_all[rs, h:h + 1]
        eg = eg_all[rs, G_HB + h:G_HB + h + 1]
        decays.append(jnp.where(causal, jnp.exp(jnp.where(causal, gcol - grow, 0.0)), 0.0))
        q_u, k_u, v_u = blk(q, j, h), blk(k, j, h), blk(v, j, h)
        q_u = q_u * lax.rsqrt(jnp.sum(q_u * q_u, axis=-1, keepdims=True) + 1e-6) * (G_HEAD ** -0.5)
        k_u = k_u * lax.rsqrt(jnp.sum(k_u * k_u, axis=-1, keepdims=True) + 1e-6)
        kb = k_u * beta
        qs.append(q_u)
        ks.append(k_u)
        kbs.append(kb)
        rhs.append(jnp.concatenate([v_u * beta, kb * eg], axis=1))
        egs.append(eg)
        gcols.append(gcol)
    s_old = [s_ref[j, h] for j, h in units]
    amats = [_mm(jnp.concatenate([kb, q_u], axis=0), k_u, NT, PASSES_MAIN)
             for kb, q_u, k_u in zip(kbs, qs, ks)]
    lmats = [jnp.where(strict, am[:tb] * dc, 0.0) for am, dc in zip(amats, decays)]
    attns = [am[tb:] * dc for am, dc in zip(amats, decays)]
    tinvs = _inv_unit_lower([-lm for lm in lmats], levels, G_PASSES_INV)
    uws = [_mm(t, rr, NN, PASSES_MAIN) for t, rr in zip(tinvs, rhs)]
    xss = [_mm(jnp.concatenate([uw[:, G_HEAD:], q_u * eg], axis=0), s, NN, PASSES_MAIN)
           for uw, q_u, eg, s in zip(uws, qs, egs, s_old)]
    vnews = [uw[:, :G_HEAD] - xs[:tb] for uw, xs in zip(uws, xss)]
    os_ = [xs[tb:] + _mm(at, vn, NN, PASSES_MAIN) for xs, at, vn in zip(xss, attns, vnews)]
    ktails = [k_u * jnp.exp(gc[tb - 1:tb, :] - gc) for k_u, gc in zip(ks, gcols)]
    upd = [_mm(kt, vn, TN, PASSES_MAIN) for kt, vn in zip(ktails, vnews)]

    out_rows = []
    for j in range(sb):
        outs = []
        for h in range(nh):
            i = j * nh + h
            s_ref[j, h] = s_old[i] * egs[i][tb - 1:tb, :] + upd[i]
            o = os_[i]
            o = o * lax.rsqrt(jnp.mean(o * o, axis=-1, keepdims=True) + NORM_EPS) * nw
            outs.append(o)
        out_rows.append(jnp.concatenate(outs, axis=1))
    out = out_rows[0] if sb == 1 else jnp.concatenate(out_rows, axis=0)
    yb_ref[...] = (out * _silu(z)).reshape(sb, tb, HBW).astype(yb_ref.dtype)

    @pl.when(ci == pl.num_programs(2) - 1)
    def _():
        sout_ref[...] = s_ref[...]


def _gdn(p_g, p_ba, nb, t_pad, t_valid, gw, prm, state):
    tb = min(CHUNK, t_pad)
    nc = t_pad // tb
    sb = _seq_block(nb, t_pad, nc, G_LONG_SEQ_BLOCK)
    nhb = gw // HBW
    heads = gw // G_HEAD
    qb = 0
    kb, vb, zb = gw // HBW, 2 * gw // HBW, 3 * gw // HBW
    bab = 0
    has_state = state is not None
    levels = max(1, (min(tb, t_valid or tb) - 1).bit_length())
    yb_dtype = BF16 if tb % (2 * SUBLANES) == 0 else F32

    def pcol(off):
        return pl.BlockSpec((sb, tb, HBW), lambda b, h, ci: (b, ci, off + h))

    def cwspec(off):
        return pl.BlockSpec((G_CONV, HBW), lambda b, h, ci: (0, off + h))

    in_specs = [pcol(qb), pcol(kb), pcol(vb), pcol(zb),
                pl.BlockSpec((sb, tb, LANES), lambda b, h, ci: (b, ci, bab + h)),
                cwspec(0), cwspec(gw // HBW), cwspec(2 * gw // HBW),
                pl.BlockSpec((1, LANES), lambda b, h, ci: (0, h)),
                pl.BlockSpec((1, LANES), lambda b, h, ci: (0, h)),
                pl.BlockSpec((1, G_HEAD), lambda b, h, ci: (0, 0))]
    args = [p_g, p_g, p_g, p_g, p_ba, prm["conv_w"], prm["conv_w"], prm["conv_w"],
            prm["neg_a"], prm["dt"], prm["norm_w"]]
    state_spec = pl.BlockSpec((sb, G_HB, G_HEAD, G_HEAD), lambda b, h, ci: (b, h, 0, 0))
    if has_state:
        s0, hist = state
        in_specs += [state_spec,
                     pl.BlockSpec((sb, SUBLANES, HBW), lambda b, h, ci: (b, 0, h)),
                     pl.BlockSpec((sb, SUBLANES, HBW), lambda b, h, ci: (b, 0, gw // HBW + h)),
                     pl.BlockSpec((sb, SUBLANES, HBW), lambda b, h, ci: (b, 0, 2 * gw // HBW + h))]
        args += [s0, hist, hist, hist]
    kern = functools.partial(_gdn_kernel, tb=tb, sb=sb, t_valid=t_valid, has_state=has_state,
                             levels=levels)
    return pl.pallas_call(
        kern,
        grid=(nb // sb, nhb, nc),
        in_specs=in_specs,
        out_specs=[pl.BlockSpec((sb, tb, HBW), lambda b, h, ci: (b, ci, h)), state_spec],
        out_shape=[jax.ShapeDtypeStruct((nb, t_pad, gw), yb_dtype),
                   jax.ShapeDtypeStruct((nb, heads, G_HEAD, G_HEAD), F32)],
        scratch_shapes=[pltpu.VMEM((sb, G_HB, G_HEAD, G_HEAD), F32),
                        pltpu.VMEM((3, sb, SUBLANES, HBW), F32)],
        compiler_params=_cparams(("parallel", "parallel", "arbitrary")),
        name="gdn_chunk",
    )(*args)


def _run_group(x, mods, nb, t, lay, wts, rprm, gprm, states):
    m, d = x.shape
    rw, gw = lay["rw"], lay["gw"]
    tm = _pick(m, (512, 256, 128, 64, 32, 16, 8))
    tm_proj = _pick(m, (1024, 512, 256, 128, 64, 32, 16, 8))
    if mods.per_seq:
        tm = _pick(t, (512, 256, 128, 64, 32, 16, 8))
        tm_proj = _pick(t, (1024, 512, 256, 128, 64, 32, 16, 8))
    tm_norm = min(tm, 256)
    w_in = wts["w_in"]

    h = _norm_mod(x, wts["norm1_w"], mods, 0, 1, tm_norm)
    p_r = _proj(h, w_in, 0, lay["rseg"], tm_proj, F32, "in_proj_r")
    p_g = _proj(h, w_in, lay["r_cols"], 4 * gw, tm_proj, F32, "in_proj_g")
    p_m = _proj(h, w_in, lay["m_col0"], 2 * d, tm_proj, F32, "in_proj_m")
    n_ba = wts["w_ba"].shape[1]
    p_ba = _matmul(h, wts["w_ba"], tm, _pick(n_ba, (512, 256, 128)), F32, "in_proj_ba")

    p_r, p_g, p_ba = (p.reshape(nb, t, p.shape[1]) for p in (p_r, p_g, p_ba))
    shift_new = p_r[:, t - 1, :lay["r_cols"]]
    conv_new = p_g[:, t - (G_CONV - 1):, :3 * gw]

    if t % SUBLANES == 0:
        t_pad, t_valid = t, None
    else:
        t_pad = -(-t // SUBLANES) * SUBLANES
        t_valid = t
        p_r, p_g, p_ba = (jnp.pad(p, ((0, 0), (0, t_pad - t), (0, 0))) for p in (p_r, p_g, p_ba))

    ya, wkv_new = _rwkv(p_r, nb, t_pad, t_valid, rw, rprm, states and states["rwkv"])
    yb, gdn_new = _gdn(p_g, p_ba, nb, t_pad, t_valid, gw, gprm, states and states["gdn"])
    ya = ya[:, :t].reshape(m, rw).astype(BF16)
    yb = yb[:, :t].reshape(m, gw).astype(BF16)

    merged = _merge(ya, yb, wts["w_out_a"], wts["w_out_b"], p_m, tm)
    x1 = _out_resid(merged, wts["w_out"], x, mods, 2, tm)
    h2 = _norm_mod(x1, wts["norm2_w"], mods, 3, 4, tm_norm)
    dff = wts["w_up"].shape[1]
    up = _proj(h2, wts["w_up"], 0, dff, tm_proj, BF16, "up_proj", _relu2)
    y = _down_final(up, wts["w_down"], x1, mods, 5, wts["final_norm_w"], tm,
                    _pick(dff, (1024, 512, 256, 128)))
    return y, wkv_new, shift_new, gdn_new, conv_new


def kernel(x_prompt, x_sample, state_rwkv_wkv, state_rwkv_shift, state_gdn, state_gdn_conv,
           c_prompt, c_sample, norm1_w, norm2_w, w_ada, b_ada, w_in, r_mu, r_w0, r_w_w2, r_a0,
           r_w_a2, r_w_g2, r_k_k, r_k_a, r_r_k, r_lnx_w, r_lnx_b, g_conv_w, g_a_log, g_dt_bias,
           g_norm_w, w_out_a, w_out_b, w_out, w_up, w_down, final_norm_w):
    depth = w_in.shape[0]
    assert depth == 1, "single-layer trunk"
    bp, tp, d = x_prompt.shape
    bs, ts, _ = x_sample.shape
    rw = r_w0.shape[1]
    gw = g_conv_w.shape[2] // 3
    g_heads = g_a_log.shape[1]
    dlw, dla, dlg = r_w_w2.shape[1], r_w_a2.shape[1], r_w_g2.shape[1]
    lora = dlw + dla + dlg
    r_cols = 3 * rw + lora
    g_cols = 4 * gw + 2 * g_heads
    assert rw % HBW == 0 and gw % HBW == 0 and lora <= HBW and g_heads % G_HB == 0
    assert w_in.shape[2] == r_cols + g_cols + 2 * d

    assert d % PROJ_TN == 0 and (3 * rw + HBW) % PROJ_TN == 0 and (4 * gw) % PROJ_TN == 0
    rseg = 3 * rw + HBW
    n_ba = (g_heads // G_HB) * LANES
    lay = dict(rw=rw, gw=gw, r_cols=r_cols, rseg=rseg, m_col0=r_cols + g_cols)

    wi = w_in[0]
    w_b = wi[:, r_cols + 4 * gw:r_cols + 4 * gw + g_heads].reshape(d, g_heads // G_HB, G_HB)
    w_a = wi[:, r_cols + 4 * gw + g_heads:r_cols + g_cols].reshape(d, g_heads // G_HB, G_HB)
    w_ba = jnp.concatenate([w_b, w_a, jnp.zeros((d, g_heads // G_HB, LANES - 2 * G_HB), F32)],
                           axis=2).reshape(d, n_ba).astype(BF16)

    wts = dict(w_in=wi, w_ba=w_ba, norm1_w=norm1_w[0], norm2_w=norm2_w[0],
               w_out_a=w_out_a[0], w_out_b=w_out_b[0], w_out=w_out[0], w_up=w_up[0],
               w_down=w_down[0].astype(BF16), final_norm_w=final_norm_w)

    row = lambda a: a.reshape(1, -1)

    def hilo(w):
        hi = w.astype(BF16)
        return jnp.concatenate([hi, (w - hi.astype(F32)).astype(BF16)], axis=0)

    rprm = dict(mu=jnp.pad(row(r_mu[0]), ((0, 0), (0, rseg - r_cols))),
                w0=row(r_w0[0]), a0=row(r_a0[0]), k_k=row(r_k_k[0]), k_a=row(r_k_a[0]),
                r_k=row(r_r_k[0]), lnx_w=row(r_lnx_w[0]), lnx_b=row(r_lnx_b[0]),
                ww2=hilo(r_w_w2[0]), wa2=hilo(r_w_a2[0]), wg2=hilo(r_w_g2[0]))

    def ba_row(vals):
        v = vals.reshape(g_heads // G_HB, G_HB)
        z = jnp.zeros_like(v)
        return jnp.concatenate([z, v, jnp.zeros((g_heads // G_HB, LANES - 2 * G_HB), F32)],
                               axis=1).reshape(1, n_ba)

    gprm = dict(conv_w=g_conv_w[0], neg_a=ba_row(-jnp.exp(g_a_log[0])), dt=ba_row(g_dt_bias[0]),
                norm_w=row(g_norm_w[0]))

    c_all = jnp.concatenate([c_prompt, c_sample], axis=0)
    n_seq = bp + bs
    c_all = jnp.pad(c_all, ((0, -n_seq % SUBLANES), (0, 0)))
    mods_all = _adaln(c_all, w_ada[0], b_ada[0])
    mods_p = _Mods(mods_all[:bp].reshape(bp, 1, 6 * d), True, d, tp)
    mods_s = _Mods(jnp.repeat(mods_all[bp:n_seq], ts, axis=0), False, d, ts)

    y_p, p_wkv, p_shift, p_gdn, p_conv = _run_group(
        x_prompt.reshape(bp * tp, d), mods_p, bp, tp, lay, wts, rprm, gprm, None)

    shift0 = jnp.pad(state_rwkv_shift[0], ((0, 0), (0, rseg - r_cols))).reshape(bs, 1, rseg)
    hist0 = jnp.pad(state_gdn_conv[0], ((0, 0), (SUBLANES - (G_CONV - 1), 0), (0, 0)))
    states = dict(rwkv=(state_rwkv_wkv[0], shift0), gdn=(state_gdn[0], hist0))
    y_s, s_wkv, s_shift, s_gdn, s_conv = _run_group(
        x_sample.reshape(bs * ts, d), mods_s, bs, ts, lay, wts, rprm, gprm, states)

    return (y_p.reshape(bp, tp, d), y_s.reshape(bs, ts, d),
            p_wkv[None], p_shift[None], p_gdn[None], p_conv[None],
            s_wkv[None], s_shift[None], s_gdn[None], s_conv[None])
```

```python
import functools

import jax
import jax.numpy as jnp
from jax import lax
from jax.experimental import pallas as pl
from jax.experimental.pallas import tpu as pltpu

F32 = jnp.float32
BF16 = jnp.bfloat16

LANES = 128
SUBLANES = 8
VMEM_LIMIT = 48 * 1024 * 1024
VMEM_LIMIT_WIDE = 56 * 1024 * 1024

R_HEAD = 64
G_HEAD = 128
G_CONV = 4
R_HB = 8
G_HB = 4
HBW = 512
CHUNK = 64
REPACK_TN = 512
SEQ_BLOCK_ROWS = 64
R_LONG_SEQ_BLOCK = 4
G_LONG_SEQ_BLOCK = 4
NORM_EPS = 1e-6
RWKV_GN_EPS = 64e-5
PASSES_MAIN = 1
R_PASSES_INV = 3
G_PASSES_INV = 3

NN = ((1,), (0,))
NT = ((1,), (1,))
TN = ((0,), (0,))


def _cparams(sem, vmem=VMEM_LIMIT):
    return pltpu.CompilerParams(dimension_semantics=sem, vmem_limit_bytes=vmem)


def _pick(n, cands):
    for c in cands:
        if n % c == 0:
            return c
    raise ValueError(f"no tile in {cands} divides {n}")


def _dot(a, b, dims=NN):
    return lax.dot_general(a, b, (dims, ((), ())), preferred_element_type=F32)


def _mm1(a, b, dims=NN):
    return _dot(a.astype(BF16), b.astype(BF16), dims)


def _split2(a):
    hi = a.astype(BF16)
    lo = (a - hi.astype(F32)).astype(BF16)
    return hi, lo


def _mm3(a, b, dims=NN):
    ah, al = _split2(a)
    bh, bl = _split2(b)
    if dims == TN:
        return _dot(ah, bh, dims) + (_dot(ah, bl, dims) + _dot(al, bh, dims))
    m = a.shape[0]
    ah32 = ah.astype(F32)
    stacked = jnp.concatenate([ah32, a - ah32], axis=0).astype(BF16)
    r = _dot(stacked, bh, dims)
    return r[:m] + (_dot(ah, bl, dims) + r[m:])


def _mm(a, b, dims=NN, passes=1):
    return _mm1(a, b, dims) if passes == 1 else _mm3(a, b, dims)


def _mm3_presplit(a, b_hl):
    m = a.shape[0]
    kdim = b_hl.shape[0] // 2
    ah = a.astype(BF16)
    ah32 = ah.astype(F32)
    stacked = jnp.concatenate([ah32, a - ah32], axis=0).astype(BF16)
    r = _dot(stacked, b_hl[:kdim])
    return r[:m] + (_dot(ah, b_hl[kdim:]) + r[m:])


def _mm_exact_lhs(a_bf16, b, dims=NN):
    b1 = b.astype(BF16)
    r1 = b - b1.astype(F32)
    b2 = r1.astype(BF16)
    b3 = (r1 - b2.astype(F32)).astype(BF16)
    return _dot(a_bf16, b1, dims) + (_dot(a_bf16, b2, dims) + _dot(a_bf16, b3, dims))


def _sigmoid(x):
    return 1.0 / (1.0 + jnp.exp(-x))


def _silu(x):
    return x * _sigmoid(x)


def _softplus(x):
    return jnp.maximum(x, 0.0) + jnp.log1p(jnp.exp(-jnp.abs(x)))


def _tri_masks(c, reps=1):
    row = lax.broadcasted_iota(jnp.int32, (c, reps * c), 0)
    col = lax.broadcasted_iota(jnp.int32, (c, reps * c), 1) % c
    return row > col, row >= col


def _block_cumsum_matrix(rows, tb):
    row = lax.broadcasted_iota(jnp.int32, (rows, rows), 0)
    col = lax.broadcasted_iota(jnp.int32, (rows, rows), 1)
    keep = (row >= col) & (row // tb == col // tb)
    return jnp.where(keep, 1.0, 0.0).astype(BF16)


INV_PACK = 4


def _inv_unit_lower(ns, levels, passes):
    c = ns[0].shape[0]
    pack = INV_PACK if (c * INV_PACK == 2 * LANES and len(ns) % INV_PACK == 0) else 1
    width = pack * c
    row = lax.broadcasted_iota(jnp.int32, (c, width), 0)
    col = lax.broadcasted_iota(jnp.int32, (c, width), 1)
    eye = jnp.where(row == col % c, 1.0, 0.0).astype(F32)
    blk_of_lane = lax.broadcasted_iota(jnp.int32, (1, width), 1) // c

    def bdiag(w):
        if pack == 1:
            return w
        zero = jnp.zeros_like(w)
        return jnp.concatenate([jnp.where(blk_of_lane == i, w, zero) for i in range(pack)], axis=0)

    def mul(lhs, p):
        if passes == 1:
            return _dot(lhs.astype(BF16), bdiag(p.astype(BF16)))
        m = lhs.shape[0]
        ph = p.astype(BF16)
        pl_ = (p - ph.astype(F32)).astype(BF16)
        lh = lhs.astype(BF16)
        lh32 = lh.astype(F32)
        stacked = jnp.concatenate([lh32, lhs - lh32], axis=0).astype(BF16)
        r = _dot(stacked, bdiag(ph))
        return r[:m] + (_dot(lh, bdiag(pl_)) + r[m:])

    packed = [ns[q] if pack == 1 else jnp.concatenate(ns[q:q + pack], axis=1)
              for q in range(0, len(ns), pack)]
    ts = [eye + n for n in packed]
    if levels > 1:
        ps = [mul(n, n) for n in packed]
        for lvl in range(1, levels):
            if lvl == levels - 1:
                ts = [t + mul(t, p) for t, p in zip(ts, ps)]
            else:
                rs = [mul(jnp.concatenate([t, p], axis=0), p) for t, p in zip(ts, ps)]
                ts = [t + r[:c] for t, r in zip(ts, rs)]
                ps = [r[c:] for r in rs]
    if pack == 1:
        return ts
    return [t[:, i * c:(i + 1) * c] for t in ts for i in range(pack)]


def _shift_rows(x, hists, k, tb):
    rows = x.shape[0]
    xr = pltpu.roll(x, k, 0)
    hr = [pltpu.roll(h, k, 0) for h in hists]
    if tb == SUBLANES:
        hfull = hr[0] if len(hr) == 1 else jnp.concatenate(hr, axis=0)
        row = lax.broadcasted_iota(jnp.int32, (rows, 1), 0)
        return jnp.where(row % tb < k, hfull, xr)
    row8 = lax.broadcasted_iota(jnp.int32, (SUBLANES, 1), 0)
    pieces = []
    for j, h in enumerate(hr):
        pieces.append(jnp.where(row8 < k, h, xr[j * tb:j * tb + SUBLANES]))
        pieces.append(xr[j * tb + SUBLANES:(j + 1) * tb])
    return jnp.concatenate(pieces, axis=0)


def _rows_from(parts, tb):
    tiles = [jnp.broadcast_to(p, (tb, p.shape[1])) for p in parts]
    return tiles[0] if len(tiles) == 1 else jnp.concatenate(tiles, axis=0)


def _adaln_kernel(c_ref, w_ref, b_ref, o_ref):
    c = c_ref[...]
    s = _silu(c)
    o_ref[...] = _mm1(s, w_ref[...]) + b_ref[...]


def _adaln(c_all, w_ada, b_ada):
    m, d = c_all.shape
    n = w_ada.shape[1]
    tn = _pick(n, (512, 256, 128))
    return pl.pallas_call(
        _adaln_kernel,
        grid=(n // tn,),
        in_specs=[pl.BlockSpec((m, d), lambda j: (0, 0)),
                  pl.BlockSpec((d, tn), lambda j: (0, j)),
                  pl.BlockSpec((1, tn), lambda j: (0, j))],
        out_specs=pl.BlockSpec((m, tn), lambda j: (0, j)),
        out_shape=jax.ShapeDtypeStruct((m, n), F32),
        compiler_params=_cparams(("parallel",)),
        name="adaln",
    )(c_all, w_ada, b_ada.reshape(1, n))


class _Mods:
    def __init__(self, arr, d, rows_per_seq):
        self.arr, self.d, self.rows_per_seq = arr, d, rows_per_seq

    def spec(self, tm, tn, which, ij):
        off = which * self.d // tn
        if self.arr.ndim == 3:
            tps = self.rows_per_seq // tm
            return pl.BlockSpec((1, 1, tn), lambda *g: (ij(*g)[0] // tps, 0, off + ij(*g)[1]))
        return pl.BlockSpec((tm // self.rows_per_seq, tn), lambda *g: (ij(*g)[0], off + ij(*g)[1]))


def _mod_val(ref, rows):
    v = ref[...]
    if v.ndim == 3:
        return v[0]
    seqs = v.shape[0]
    r = lax.broadcasted_iota(jnp.int32, (rows, seqs), 0) // (rows // seqs)
    c = lax.broadcasted_iota(jnp.int32, (rows, seqs), 1)
    return _mm_exact_lhs(jnp.where(r == c, 1.0, 0.0).astype(BF16), v)


def _norm_mod_kernel(x_ref, w_ref, sc_ref, sh_ref, o_ref):
    x = x_ref[...]
    y = x * lax.rsqrt(jnp.mean(x * x, axis=-1, keepdims=True) + NORM_EPS) * w_ref[...]
    rows = x.shape[0]
    o_ref[...] = (y * (1.0 + _mod_val(sc_ref, rows)) + _mod_val(sh_ref, rows)).astype(o_ref.dtype)


def _norm_mod(x, w, mods, sh_idx, sc_idx, tm):
    m, d = x.shape
    ij = lambda i: (i, 0)
    return pl.pallas_call(
        _norm_mod_kernel,
        grid=(m // tm,),
        in_specs=[pl.BlockSpec((tm, d), lambda i: (i, 0)),
                  pl.BlockSpec((1, d), lambda i: (0, 0)),
                  mods.spec(tm, d, sc_idx, ij),
                  mods.spec(tm, d, sh_idx, ij)],
        out_specs=pl.BlockSpec((tm, d), lambda i: (i, 0)),
        out_shape=jax.ShapeDtypeStruct((m, d), BF16),
        compiler_params=_cparams(("parallel",)),
        name="norm_mod",
    )(x, w.reshape(1, d), mods.arr, mods.arr)


def _mm_kernel(a_ref, w_ref, o_ref):
    o_ref[...] = _dot(a_ref[...], w_ref[...]).astype(o_ref.dtype)


def _matmul(a, w, tm, tn, out_dtype, name):
    m, k = a.shape
    n = w.shape[1]
    return pl.pallas_call(
        _mm_kernel,
        grid=(n // tn, m // tm),
        in_specs=[pl.BlockSpec((tm, k), lambda j, i: (i, 0)),
                  pl.BlockSpec((k, tn), lambda j, i: (0, j))],
        out_specs=pl.BlockSpec((tm, tn), lambda j, i: (i, j)),
        out_shape=jax.ShapeDtypeStruct((m, n), out_dtype),
        compiler_params=_cparams(("parallel", "parallel")),
        name=name,
    )(a, w)


def _repack_kernel(*refs, off):
    o_ref = refs[-1]
    w = refs[0][...]
    if off:
        w = jnp.concatenate([w, refs[1][...]], axis=1)[:, off:off + REPACK_TN]
    o_ref[...] = w.astype(o_ref.dtype)


def _repack(w, col0, n_cols, out, out_col0, n_total):
    k = w.shape[0]
    rk = _pick(k, (1024, 512, 256, 128))
    base, off = divmod(col0, REPACK_TN)
    tile0 = out_col0 // REPACK_TN
    in_specs = [pl.BlockSpec((rk, REPACK_TN), lambda j, r: (r, base + j))]
    args = [w]
    if off:
        in_specs.append(pl.BlockSpec((rk, REPACK_TN), lambda j, r: (r, base + j + 1)))
        args.append(w)
    aliases = {}
    if out is not None:
        aliases = {len(args): 0}
        in_specs.append(pl.BlockSpec(memory_space=pl.ANY))
        args.append(out)
    return pl.pallas_call(
        functools.partial(_repack_kernel, off=off),
        grid=(n_cols // REPACK_TN, k // rk),
        in_specs=in_specs,
        out_specs=pl.BlockSpec((rk, REPACK_TN), lambda j, r: (r, tile0 + j)),
        out_shape=jax.ShapeDtypeStruct((k, n_total), BF16),
        input_output_aliases=aliases,
        compiler_params=_cparams(("parallel", "parallel")),
        name="repack_w_in",
    )(*args)


def _relu2_kernel(a_ref, w_ref, o_ref):
    u = jnp.maximum(_dot(a_ref[...], w_ref[...]), 0.0)
    o_ref[...] = (u * u).astype(o_ref.dtype)


def _up_proj(a, w, tm, tn):
    m, k = a.shape
    n = w.shape[1]
    return pl.pallas_call(
        _relu2_kernel,
        grid=(n // tn, m // tm),
        in_specs=[pl.BlockSpec((tm, k), lambda j, i: (i, 0)),
                  pl.BlockSpec((k, tn), lambda j, i: (0, j))],
        out_specs=pl.BlockSpec((tm, tn), lambda j, i: (i, j)),
        out_shape=jax.ShapeDtypeStruct((m, n), BF16),
        compiler_params=_cparams(("parallel", "parallel")),
        name="up_proj",
    )(a, w)


def _merge_kernel(ya_ref, yb_ref, wa_ref, wb_ref, ga_ref, gb_ref, o_ref):
    pa = _dot(ya_ref[...], wa_ref[...])
    pb = _dot(yb_ref[...], wb_ref[...])
    o_ref[...] = (_sigmoid(ga_ref[...]) * pa + _sigmoid(gb_ref[...]) * pb).astype(o_ref.dtype)


def _merge(ya, yb, wa, wb, p_all, moff, tm, tn):
    m, ka = ya.shape
    kb = yb.shape[1]
    d = wa.shape[1]
    oa, ob = moff // tn, (moff + d) // tn
    return pl.pallas_call(
        _merge_kernel,
        grid=(d // tn, m // tm),
        in_specs=[pl.BlockSpec((tm, ka), lambda j, i: (i, 0)),
                  pl.BlockSpec((tm, kb), lambda j, i: (i, 0)),
                  pl.BlockSpec((ka, tn), lambda j, i: (0, j)),
                  pl.BlockSpec((kb, tn), lambda j, i: (0, j)),
                  pl.BlockSpec((tm, tn), lambda j, i: (i, oa + j)),
                  pl.BlockSpec((tm, tn), lambda j, i: (i, ob + j))],
        out_specs=pl.BlockSpec((tm, tn), lambda j, i: (i, j)),
        out_shape=jax.ShapeDtypeStruct((m, d), BF16),
        compiler_params=_cparams(("parallel", "parallel")),
        name="merge",
    )(ya, yb, wa, wb, p_all, p_all)


def _resid_kernel(a_ref, w_ref, x_ref, g_ref, o_ref):
    o_ref[...] = x_ref[...] + _mod_val(g_ref, x_ref.shape[0]) * _dot(a_ref[...], w_ref[...])


def _out_resid(a, w, x, mods, gate_idx, tm, tn):
    m, k = a.shape
    n = w.shape[1]
    ij = lambda j, i: (i, j)
    return pl.pallas_call(
        _resid_kernel,
        grid=(n // tn, m // tm),
        in_specs=[pl.BlockSpec((tm, k), lambda j, i: (i, 0)),
                  pl.BlockSpec((k, tn), lambda j, i: (0, j)),
                  pl.BlockSpec((tm, tn), lambda j, i: (i, j)),
                  mods.spec(tm, tn, gate_idx, ij)],
        out_specs=pl.BlockSpec((tm, tn), lambda j, i: (i, j)),
        out_shape=jax.ShapeDtypeStruct((m, n), F32),
        compiler_params=_cparams(("parallel", "parallel")),
        name="out_resid",
    )(a, w, x, mods.arr)


def _down_kernel(a_ref, w_ref, x_ref, g_ref, fw_ref, o_ref):
    kk = pl.program_id(1)

    @pl.when(kk == 0)
    def _():
        o_ref[...] = jnp.zeros_like(o_ref)

    o_ref[...] = _dot(a_ref[...], w_ref[...]) + o_ref[...]

    @pl.when(kk == pl.num_programs(1) - 1)
    def _():
        x2 = x_ref[...] + _mod_val(g_ref, o_ref.shape[0]) * o_ref[...]
        y = x2 * lax.rsqrt(jnp.mean(x2 * x2, axis=-1, keepdims=True) + NORM_EPS)
        o_ref[...] = y * fw_ref[...]


def _down_final(a, w, x, mods, gate_idx, final_w, tm, tk):
    m, k = a.shape
    n = w.shape[1]
    ij = lambda i, kk: (i, 0)
    return pl.pallas_call(
        _down_kernel,
        grid=(m // tm, k // tk),
        in_specs=[pl.BlockSpec((tm, tk), lambda i, kk: (i, kk)),
                  pl.BlockSpec((tk, n), lambda i, kk: (kk, 0)),
                  pl.BlockSpec((tm, n), lambda i, kk: (i, 0), pipeline_mode=pl.Buffered(1)),
                  mods.spec(tm, n, gate_idx, ij),
                  pl.BlockSpec((1, n), lambda i, kk: (0, 0))],
        out_specs=pl.BlockSpec((tm, n), lambda i, kk: (i, 0), pipeline_mode=pl.Buffered(1)),
        out_shape=jax.ShapeDtypeStruct((m, n), F32),
        compiler_params=_cparams(("parallel", "arbitrary"), VMEM_LIMIT_WIDE),
        name="down_final",
    )(a, w, x, mods.arr, final_w.reshape(1, n))


def _rwkv_kernel(*refs, tb, sb, t_valid, has_state, levels):
    (r_ref, k_ref, v_ref, lo_ref, mur_ref, muk_ref, muv_ref, mul_ref,
     w0_ref, a0_ref, kk_ref, ka_ref, rk_ref, lw_ref, lb_ref,
     ww2_ref, wa2_ref, wg2_ref) = refs[:18]
    n = 18
    if has_state:
        s0_ref, shr_ref, shk_ref, shv_ref, shl_ref = refs[n:n + 5]
        n += 5
    ya_ref, sout_ref, s_ref, prev_ref = refs[n:n + 4]
    ci = pl.program_id(2)
    rows = sb * tb
    nh = HBW // R_HEAD

    @pl.when(ci == 0)
    def _():
        if has_state:
            s_ref[...] = s0_ref[...]
            for slot, sh in enumerate((shr_ref, shk_ref, shv_ref, shl_ref)):
                for j in range(sb):
                    prev_ref[slot, j, 0:1, :] = sh[j]
        else:
            s_ref[...] = jnp.zeros_like(s_ref)
            prev_ref[...] = jnp.zeros_like(prev_ref)

    row = lax.broadcasted_iota(jnp.int32, (rows, 1), 0)
    is_first = row % tb == 0

    def tshift(x_ref, slot, mu_ref):
        x = x_ref[...].reshape(rows, HBW)
        first = _rows_from([prev_ref[slot, j, 0:1, :] for j in range(sb)], tb)
        prev = jnp.where(is_first, first, pltpu.roll(x, 1, 0))
        for j in range(sb):
            prev_ref[slot, j, 0:1, :] = x[(j + 1) * tb - 1:(j + 1) * tb, :]
        return x + (prev - x) * mu_ref[...]

    r = tshift(r_ref, 0, mur_ref)
    k = tshift(k_ref, 1, muk_ref)
    v = tshift(v_ref, 2, muv_ref)
    lo = tshift(lo_ref, 3, mul_ref)
    dlw = ww2_ref.shape[0] // 2
    dla = wa2_ref.shape[0] // 2
    dlg = wg2_ref.shape[0] // 2
    dw = lo[:, 0:dlw]
    da = lo[:, dlw:dlw + dla]
    dg = lo[:, dlw + dla:dlw + dla + dlg]

    log_w = -_softplus(-(w0_ref[...] + _mm3_presplit(jnp.tanh(dw), ww2_ref[...]))) - 0.5
    lw = -jnp.exp(log_w)
    a = _sigmoid(a0_ref[...] + _mm3_presplit(da, wa2_ref[...]))
    g = _mm3_presplit(_sigmoid(dg), wg2_ref[...])

    if t_valid is not None:
        valid = (row % tb < t_valid).astype(F32)
        lw = lw * valid
        k = k * valid
        v = v * valid

    cum = _mm_exact_lhs(_block_cumsum_matrix(rows, tb), lw)
    w_inc = jnp.exp(cum)
    w_inv = jnp.exp(-cum)
    w_exc = jnp.exp(cum - lw)
    rp = r * w_inc
    kx = k * kk_ref[...]
    k2 = k * (1.0 + (a - 1.0) * ka_ref[...])
    kp = k2 * w_inv
    rk2 = r * k2 * rk_ref[...]
    aw_inv = a * w_inv

    strict2, causal2 = _tri_masks(tb, 2)
    units = [(j, h) for j in range(sb) for h in range(nh)]

    def blk(x, j, h):
        return x[j * tb:(j + 1) * tb, h * R_HEAD:(h + 1) * R_HEAD]

    xs_, ys_, vs_ = [], [], []
    for j, h in units:
        kx_u = blk(kx, j, h)
        kk_u = kx_u * lax.rsqrt(jnp.sum(kx_u * kx_u, axis=-1, keepdims=True) + 1e-6)
        ap = -kk_u * blk(w_exc, j, h)
        bp = kk_u * blk(aw_inv, j, h)
        xs_.append(jnp.concatenate([ap, blk(rp, j, h)], axis=0))
        ys_.append(jnp.concatenate([bp, blk(kp, j, h)], axis=0))
        vs_.append(blk(v, j, h))
    s_old = [s_ref[j, h] for j, h in units]
    amats = [_mm(x, y, NT, PASSES_MAIN) for x, y in zip(xs_, ys_)]
    xss = [_mm(x, s, NT, PASSES_MAIN) for x, s in zip(xs_, s_old)]
    tops = [jnp.where(strict2, am[:tb], 0.0) for am in amats]
    bots = [jnp.where(causal2, am[tb:], 0.0) for am in amats]
    tinvs = _inv_unit_lower([tp[:, :tb] for tp in tops], levels, R_PASSES_INV)
    rhs = [xs[:tb] + _mm(tp[:, tb:], vv, NN, PASSES_MAIN) for xs, tp, vv in zip(xss, tops, vs_)]
    us = [_mm(t, rr, NN, PASSES_MAIN) for t, rr in zip(tinvs, rhs)]
    uvs = [jnp.concatenate([u, vv], axis=0) for u, vv in zip(us, vs_)]
    os_ = [xs[tb:] + _mm(bt, uv, NN, PASSES_MAIN) for xs, bt, uv in zip(xss, bots, uvs)]
    upd = [_mm(uv, y, TN, PASSES_MAIN) for uv, y in zip(uvs, ys_)]
    for i, (j, h) in enumerate(units):
        s_ref[j, h] = (s_old[i] + upd[i]) * blk(w_inc, j, h)[tb - 1:tb, :]

    lnw = lw_ref[...]
    lnb = lb_ref[...]
    out_rows = []
    for j in range(sb):
        outs = []
        for h in range(nh):
            o = os_[j * nh + h]
            mu = jnp.mean(o, axis=-1, keepdims=True)
            var = jnp.mean(jnp.square(o - mu), axis=-1, keepdims=True)
            sl = slice(h * R_HEAD, (h + 1) * R_HEAD)
            on = (o - mu) * lax.rsqrt(var + RWKV_GN_EPS) * lnw[:, sl] + lnb[:, sl]
            bonus = jnp.sum(blk(rk2, j, h), axis=-1, keepdims=True) * vs_[j * nh + h]
            outs.append(on + bonus)
        out_rows.append(jnp.concatenate(outs, axis=1))
    out = out_rows[0] if sb == 1 else jnp.concatenate(out_rows, axis=0)
    ya_ref[...] = (out * g).reshape(sb, tb, HBW).astype(ya_ref.dtype)

    @pl.when(ci == pl.num_programs(2) - 1)
    def _():
        sout_ref[...] = s_ref[...]


def _seq_block(nb, t_pad, nc, long_seq_block):
    sb = long_seq_block if nc > 1 else max(1, SEQ_BLOCK_ROWS // t_pad)
    while nb % sb:
        sb -= 1
    return sb


def _rwkv(p_all, nb, t_pad, t_valid, rw, prm, state):
    tb = min(CHUNK, t_pad)
    nc = t_pad // tb
    sb = _seq_block(nb, t_pad, nc, R_LONG_SEQ_BLOCK)
    nhb = rw // HBW
    heads = rw // R_HEAD
    kb, vb, lb = rw // HBW, 2 * rw // HBW, 3 * rw // HBW
    has_state = state is not None
    levels = max(1, (min(tb, t_valid or tb) - 1).bit_length())
    ya_dtype = BF16 if tb % (2 * SUBLANES) == 0 else F32

    def pcol(off):
        return pl.BlockSpec((sb, tb, HBW), lambda b, h, ci: (b, ci, off + h))

    def vec(off):
        return pl.BlockSpec((1, HBW), lambda b, h, ci: (0, off + h))

    in_specs = [pcol(0), pcol(kb), pcol(vb),
                pl.BlockSpec((sb, tb, HBW), lambda b, h, ci: (b, ci, lb)),
                vec(0), vec(kb), vec(vb),
                pl.BlockSpec((1, HBW), lambda b, h, ci: (0, lb)),
                vec(0), vec(0), vec(0), vec(0), vec(0), vec(0), vec(0),
                pl.BlockSpec((prm["ww2"].shape[0], HBW), lambda b, h, ci: (0, h)),
                pl.BlockSpec((prm["wa2"].shape[0], HBW), lambda b, h, ci: (0, h)),
                pl.BlockSpec((prm["wg2"].shape[0], HBW), lambda b, h, ci: (0, h))]
    args = [p_all, p_all, p_all, p_all, prm["mu"], prm["mu"], prm["mu"], prm["mu"],
            prm["w0"], prm["a0"], prm["k_k"], prm["k_a"], prm["r_k"], prm["lnx_w"], prm["lnx_b"],
            prm["ww2"], prm["wa2"], prm["wg2"]]
    state_spec = pl.BlockSpec((sb, R_HB, R_HEAD, R_HEAD), lambda b, h, ci: (b, h, 0, 0))
    if has_state:
        s0, shift = state
        in_specs += [state_spec,
                     pl.BlockSpec((sb, 1, HBW), lambda b, h, ci: (b, 0, h)),
                     pl.BlockSpec((sb, 1, HBW), lambda b, h, ci: (b, 0, kb + h)),
                     pl.BlockSpec((sb, 1, HBW), lambda b, h, ci: (b, 0, vb + h)),
                     pl.BlockSpec((sb, 1, HBW), lambda b, h, ci: (b, 0, lb))]
        args += [s0, shift, shift, shift, shift]
    kern = functools.partial(_rwkv_kernel, tb=tb, sb=sb, t_valid=t_valid, has_state=has_state,
                             levels=levels)
    return pl.pallas_call(
        kern,
        grid=(nb // sb, nhb, nc),
        in_specs=in_specs,
        out_specs=[pl.BlockSpec((sb, tb, HBW), lambda b, h, ci: (b, ci, h)), state_spec],
        out_shape=[jax.ShapeDtypeStruct((nb, t_pad, rw), ya_dtype),
                   jax.ShapeDtypeStruct((nb, heads, R_HEAD, R_HEAD), F32)],
        scratch_shapes=[pltpu.VMEM((sb, R_HB, R_HEAD, R_HEAD), F32),
                        pltpu.VMEM((4, sb, SUBLANES, HBW), F32)],
        compiler_params=_cparams(("parallel", "parallel", "arbitrary")),
        name="rwkv7_chunk",
    )(*args)


def _gdn_kernel(*refs, tb, sb, t_valid, has_state, levels):
    (q_ref, k_ref, v_ref, z_ref, ba_ref, cq_ref, ck_ref, cv_ref,
     an_ref, dt_ref, nw_ref) = refs[:11]
    n = 11
    if has_state:
        s0_ref, hq_ref, hk_ref, hv_ref = refs[n:n + 4]
        n += 4
    yb_ref, sout_ref, s_ref, hist_ref = refs[n:n + 4]
    ci = pl.program_id(2)
    rows = sb * tb

    @pl.when(ci == 0)
    def _():
        if has_state:
            s_ref[...] = s0_ref[...]
            hist_ref[0] = hq_ref[...]
            hist_ref[1] = hk_ref[...]
            hist_ref[2] = hv_ref[...]
        else:
            s_ref[...] = jnp.zeros_like(s_ref)
            hist_ref[...] = jnp.zeros_like(hist_ref)

    row = lax.broadcasted_iota(jnp.int32, (rows, 1), 0)

    def conv(x_ref, slot, cw_ref):
        x = x_ref[...].reshape(rows, HBW)
        hists = [hist_ref[slot, j] for j in range(sb)]
        cw = cw_ref[...]
        acc = _shift_rows(x, hists, 3, tb) * cw[0:1, :]
        acc = acc + _shift_rows(x, hists, 2, tb) * cw[1:2, :]
        acc = acc + _shift_rows(x, hists, 1, tb) * cw[2:3, :]
        acc = acc + x * cw[3:4, :]
        for j in range(sb):
            hist_ref[slot, j] = x[(j + 1) * tb - SUBLANES:(j + 1) * tb, :]
        return _silu(acc)

    q = conv(q_ref, 0, cq_ref)
    k = conv(k_ref, 1, ck_ref)
    v = conv(v_ref, 2, cv_ref)
    z = z_ref[...].reshape(rows, HBW)
    ba = ba_ref[...].reshape(rows, LANES)
    beta_all = _sigmoid(ba)
    la_all = an_ref[...] * _softplus(ba + dt_ref[...])
    if t_valid is not None:
        valid = (row % tb < t_valid).astype(F32)
        beta_all = beta_all * valid
        la_all = la_all * valid
        q = q * valid
        k = k * valid
        v = v * valid

    g_all = _mm_exact_lhs(_block_cumsum_matrix(rows, tb), la_all)
    lrow = lax.broadcasted_iota(jnp.int32, (LANES, LANES), 0)
    lcol = lax.broadcasted_iota(jnp.int32, (LANES, LANES), 1)
    eye = jnp.where(lrow == lcol, 1.0, 0.0).astype(BF16)
    g_t = _mm_exact_lhs(eye, g_all, NT)
    eg_all = jnp.exp(g_all)
    strict, causal = _tri_masks(tb)
    nw = nw_ref[...]
    nh = q.shape[1] // G_HEAD
    units = [(j, h) for j in range(sb) for h in range(nh)]

    def blk(x, j, h):
        return x[j * tb:(j + 1) * tb, h * G_HEAD:(h + 1) * G_HEAD]

    qs, ks, kbs, rhs, decays, egs, gcols = [], [], [], [], [], [], []
    for j, h in units:
        rs = slice(j * tb, (j + 1) * tb)
        gcol = g_all[rs, G_HB + h:G_HB + h + 1]
        grow = g_t[G_HB + h:G_HB + h + 1, rs]
        beta = beta_all[rs, h:h + 1]
        eg = eg_all[rs, G_HB + h:G_HB + h + 1]
        decays.append(jnp.where(causal, jnp.exp(jnp.where(causal, gcol - grow, 0.0)), 0.0))
        q_u, k_u, v_u = blk(q, j, h), blk(k, j, h), blk(v, j, h)
        q_u = q_u * lax.rsqrt(jnp.sum(q_u * q_u, axis=-1, keepdims=True) + 1e-6) * (G_HEAD ** -0.5)
        k_u = k_u * lax.rsqrt(jnp.sum(k_u * k_u, axis=-1, keepdims=True) + 1e-6)
        kb = k_u * beta
        qs.append(q_u)
        ks.append(k_u)
        kbs.append(kb)
        rhs.append(v_u * beta)
        egs.append(eg)
        gcols.append(gcol)
    s_old = [s_ref[j, h] for j, h in units]
    amats = [_mm(jnp.concatenate([kb, q_u], axis=0), k_u, NT, G_PASSES_INV)
             for kb, q_u, k_u in zip(kbs, qs, ks)]
    lmats = [jnp.where(strict, am[:tb] * dc, 0.0) for am, dc in zip(amats, decays)]
    attns = [am[tb:] * dc for am, dc in zip(amats, decays)]
    tinvs = _inv_unit_lower([-lm for lm in lmats], levels, G_PASSES_INV)
    xss = [_mm(jnp.concatenate([kb * eg, q_u * eg], axis=0), s, NN, PASSES_MAIN)
           for kb, q_u, eg, s in zip(kbs, qs, egs, s_old)]
    vnews = [_mm(t, rr - xs[:tb], NN, G_PASSES_INV) for t, rr, xs in zip(tinvs, rhs, xss)]
    os_ = [xs[tb:] + _mm(at, vn, NN, PASSES_MAIN) for xs, at, vn in zip(xss, attns, vnews)]
    ktails = [k_u * jnp.exp(gc[tb - 1:tb, :] - gc) for k_u, gc in zip(ks, gcols)]
    upd = [_mm(kt, vn, TN, PASSES_MAIN) for kt, vn in zip(ktails, vnews)]

    out_rows = []
    for j in range(sb):
        outs = []
        for h in range(nh):
            i = j * nh + h
            s_ref[j, h] = s_old[i] * egs[i][tb - 1:tb, :] + upd[i]
            o = os_[i]
            o = o * lax.rsqrt(jnp.mean(o * o, axis=-1, keepdims=True) + NORM_EPS) * nw
            outs.append(o)
        out_rows.append(jnp.concatenate(outs, axis=1))
    out = out_rows[0] if sb == 1 else jnp.concatenate(out_rows, axis=0)
    yb_ref[...] = (out * _silu(z)).reshape(sb, tb, HBW).astype(yb_ref.dtype)

    @pl.when(ci == pl.num_programs(2) - 1)
    def _():
        sout_ref[...] = s_ref[...]


def _gdn(p_all, nb, t_pad, t_valid, gw, goff, baoff, prm, state):
    tb = min(CHUNK, t_pad)
    nc = t_pad // tb
    sb = _seq_block(nb, t_pad, nc, G_LONG_SEQ_BLOCK)
    nhb = gw // HBW
    heads = gw // G_HEAD
    qb = goff // HBW
    kb, vb, zb = qb + gw // HBW, qb + 2 * gw // HBW, qb + 3 * gw // HBW
    bab = baoff // LANES
    has_state = state is not None
    levels = max(1, (min(tb, t_valid or tb) - 1).bit_length())
    yb_dtype = BF16 if tb % (2 * SUBLANES) == 0 else F32

    def pcol(off):
        return pl.BlockSpec((sb, tb, HBW), lambda b, h, ci: (b, ci, off + h))

    def cwspec(off):
        return pl.BlockSpec((G_CONV, HBW), lambda b, h, ci: (0, off + h))

    in_specs = [pcol(qb), pcol(kb), pcol(vb), pcol(zb),
                pl.BlockSpec((sb, tb, LANES), lambda b, h, ci: (b, ci, bab + h)),
                cwspec(0), cwspec(gw // HBW), cwspec(2 * gw // HBW),
                pl.BlockSpec((1, LANES), lambda b, h, ci: (0, h)),
                pl.BlockSpec((1, LANES), lambda b, h, ci: (0, h)),
                pl.BlockSpec((1, G_HEAD), lambda b, h, ci: (0, 0))]
    args = [p_all, p_all, p_all, p_all, p_all, prm["conv_w"], prm["conv_w"], prm["conv_w"],
            prm["neg_a"], prm["dt"], prm["norm_w"]]
    state_spec = pl.BlockSpec((sb, G_HB, G_HEAD, G_HEAD), lambda b, h, ci: (b, h, 0, 0))
    if has_state:
        s0, hist = state
        in_specs += [state_spec,
                     pl.BlockSpec((sb, SUBLANES, HBW), lambda b, h, ci: (b, 0, h)),
                     pl.BlockSpec((sb, SUBLANES, HBW), lambda b, h, ci: (b, 0, gw // HBW + h)),
                     pl.BlockSpec((sb, SUBLANES, HBW), lambda b, h, ci: (b, 0, 2 * gw // HBW + h))]
        args += [s0, hist, hist, hist]
    kern = functools.partial(_gdn_kernel, tb=tb, sb=sb, t_valid=t_valid, has_state=has_state,
                             levels=levels)
    return pl.pallas_call(
        kern,
        grid=(nb // sb, nhb, nc),
        in_specs=in_specs,
        out_specs=[pl.BlockSpec((sb, tb, HBW), lambda b, h, ci: (b, ci, h)), state_spec],
        out_shape=[jax.ShapeDtypeStruct((nb, t_pad, gw), yb_dtype),
                   jax.ShapeDtypeStruct((nb, heads, G_HEAD, G_HEAD), F32)],
        scratch_shapes=[pltpu.VMEM((sb, G_HB, G_HEAD, G_HEAD), F32),
                        pltpu.VMEM((3, sb, SUBLANES, HBW), F32)],
        compiler_params=_cparams(("parallel", "parallel", "arbitrary")),
        name="gdn_chunk",
    )(*args)


def _run_group(x, mods, nb, t, lay, wts, rprm, gprm, states):
    m, d = x.shape
    rw, gw = lay["rw"], lay["gw"]
    row_tiles = (1024, 512, 256, 128, 64, 32, 16, 8)
    span = t if mods.arr.ndim == 3 else m
    tm = _pick(span, row_tiles[1:])
    tm_proj = _pick(span, row_tiles)
    n_all = wts["w_all"].shape[1]
    tm_norm = min(tm, 256)

    h = _norm_mod(x, wts["norm1_w"], mods, 0, 1, tm_norm)
    p_all = _matmul(h, wts["w_all"], tm_proj, _pick(n_all, (1024, 512, 256, 128)), F32, "in_proj")

    p3 = p_all.reshape(nb, t, n_all)
    shift_new = p3[:, t - 1, :lay["r_cols"]]
    conv_new = p3[:, t - (G_CONV - 1):, lay["goff"]:lay["goff"] + 3 * gw]

    if t % SUBLANES == 0:
        t_pad, t_valid, p_rec = t, None, p3
    else:
        t_pad = -(-t // SUBLANES) * SUBLANES
        t_valid = t
        p_rec = jnp.pad(p3, ((0, 0), (0, t_pad - t), (0, 0)))

    ya, wkv_new = _rwkv(p_rec, nb, t_pad, t_valid, rw, rprm, states and states["rwkv"])
    yb, gdn_new = _gdn(p_rec, nb, t_pad, t_valid, gw, lay["goff"], lay["baoff"], gprm,
                       states and states["gdn"])
    ya = ya[:, :t].reshape(m, rw).astype(BF16)
    yb = yb[:, :t].reshape(m, gw).astype(BF16)

    tn_m = _pick(d, (1024, 512, 256, 128))
    while lay["moff"] % tn_m:
        tn_m //= 2
    merged = _merge(ya, yb, wts["w_out_a"], wts["w_out_b"], p_all, lay["moff"], tm, tn_m)
    x1 = _out_resid(merged, wts["w_out"], x, mods, 2, tm, _pick(d, (1024, 512, 256, 128)))
    h2 = _norm_mod(x1, wts["norm2_w"], mods, 3, 4, tm_norm)
    dff = wts["w_up"].shape[1]
    up = _up_proj(h2, wts["w_up"], tm_proj, _pick(dff, (1024, 512, 256, 128)))
    y = _down_final(up, wts["w_down"], x1, mods, 5, wts["final_norm_w"], tm,
                    _pick(dff, (1024, 512, 256, 128)))
    return y, wkv_new, shift_new, gdn_new, conv_new


def kernel(x_prompt, x_sample, state_rwkv_wkv, state_rwkv_shift, state_gdn, state_gdn_conv,
           c_prompt, c_sample, norm1_w, norm2_w, w_ada, b_ada, w_in, r_mu, r_w0, r_w_w2, r_a0,
           r_w_a2, r_w_g2, r_k_k, r_k_a, r_r_k, r_lnx_w, r_lnx_b, g_conv_w, g_a_log, g_dt_bias,
           g_norm_w, w_out_a, w_out_b, w_out, w_up, w_down, final_norm_w):
    depth = w_in.shape[0]
    assert depth == 1, "single-layer trunk"
    bp, tp, d = x_prompt.shape
    bs, ts, _ = x_sample.shape
    rw = r_w0.shape[1]
    gw = g_conv_w.shape[2] // 3
    g_heads = g_a_log.shape[1]
    dlw, dla, dlg = r_w_w2.shape[1], r_w_a2.shape[1], r_w_g2.shape[1]
    lora = dlw + dla + dlg
    r_cols = 3 * rw + lora
    g_cols = 4 * gw + 2 * g_heads
    assert rw % HBW == 0 and gw % HBW == 0 and lora <= HBW and g_heads % G_HB == 0
    assert w_in.shape[2] == r_cols + g_cols + 2 * d

    rseg = 3 * rw + HBW
    goff = rseg
    baoff = goff + 4 * gw
    n_ba = (g_heads // G_HB) * LANES
    moff = baoff + -(-n_ba // REPACK_TN) * REPACK_TN
    n_all = moff + 2 * d
    lay = dict(rw=rw, gw=gw, r_cols=r_cols, goff=goff, baoff=baoff, moff=moff)

    wi = w_in[0]
    w_b = wi[:, r_cols + 4 * gw:r_cols + 4 * gw + g_heads].reshape(d, g_heads // G_HB, G_HB)
    w_a = wi[:, r_cols + 4 * gw + g_heads:r_cols + g_cols].reshape(d, g_heads // G_HB, G_HB)
    w_ba = jnp.concatenate([w_b, w_a, jnp.zeros((d, g_heads // G_HB, LANES - 2 * G_HB), F32)],
                           axis=2).reshape(d, n_ba)
    w_ba = jnp.pad(w_ba, ((0, 0), (0, moff - baoff - n_ba)))
    w_all = _repack(wi, 0, rseg, None, 0, n_all)
    w_all = _repack(wi, r_cols, 4 * gw, w_all, goff, n_all)
    w_all = _repack(wi, r_cols + g_cols, 2 * d, w_all, moff, n_all)
    w_all = _repack(w_ba, 0, moff - baoff, w_all, baoff, n_all)

    wts = dict(w_all=w_all, norm1_w=norm1_w[0], norm2_w=norm2_w[0],
               w_out_a=w_out_a[0].astype(BF16), w_out_b=w_out_b[0].astype(BF16),
               w_out=w_out[0].astype(BF16), w_up=w_up[0].astype(BF16),
               w_down=w_down[0].astype(BF16), final_norm_w=final_norm_w)

    row = lambda a: a.reshape(1, -1)

    def hilo(w):
        hi = w.astype(BF16)
        return jnp.concatenate([hi, (w - hi.astype(F32)).astype(BF16)], axis=0)

    rprm = dict(mu=jnp.pad(row(r_mu[0]), ((0, 0), (0, rseg - r_cols))),
                w0=row(r_w0[0]), a0=row(r_a0[0]), k_k=row(r_k_k[0]), k_a=row(r_k_a[0]),
                r_k=row(r_r_k[0]), lnx_w=row(r_lnx_w[0]), lnx_b=row(r_lnx_b[0]),
                ww2=hilo(r_w_w2[0]), wa2=hilo(r_w_a2[0]), wg2=hilo(r_w_g2[0]))

    def ba_row(vals):
        v = vals.reshape(g_heads // G_HB, G_HB)
        z = jnp.zeros_like(v)
        return jnp.concatenate([z, v, jnp.zeros((g_heads // G_HB, LANES - 2 * G_HB), F32)],
                               axis=1).reshape(1, n_ba)

    gprm = dict(conv_w=g_conv_w[0], neg_a=ba_row(-jnp.exp(g_a_log[0])), dt=ba_row(g_dt_bias[0]),
                norm_w=row(g_norm_w[0]))

    c_all = jnp.concatenate([c_prompt, c_sample], axis=0)
    n_seq = bp + bs
    c_all = jnp.pad(c_all, ((0, -n_seq % SUBLANES), (0, 0)))
    mods_all = _adaln(c_all, w_ada[0], b_ada[0])
    mods_p = _Mods(mods_all[:bp].reshape(bp, 1, 6 * d), d, tp)
    mods_s = _Mods(mods_all[bp:n_seq], d, ts)

    y_p, p_wkv, p_shift, p_gdn, p_conv = _run_group(
        x_prompt.reshape(bp * tp, d), mods_p, bp, tp, lay, wts, rprm, gprm, None)

    shift0 = jnp.pad(state_rwkv_shift[0], ((0, 0), (0, rseg - r_cols))).reshape(bs, 1, rseg)
    hist0 = jnp.pad(state_gdn_conv[0], ((0, 0), (SUBLANES - (G_CONV - 1), 0), (0, 0)))
    states = dict(rwkv=(state_rwkv_wkv[0], shift0), gdn=(state_gdn[0], hist0))
    y_s, s_wkv, s_shift, s_gdn, s_conv = _run_group(
        x_sample.reshape(bs * ts, d), mods_s, bs, ts, lay, wts, rprm, gprm, states)

    return (y_p.reshape(bp, tp, d), y_s.reshape(bs, ts, d),
            p_wkv[None], p_shift[None], p_gdn[None], p_conv[None],
            s_wkv[None], s_shift[None], s_gdn[None], s_conv[None])
```

```python
import functools

import jax
import jax.numpy as jnp
from jax import lax
from jax.experimental import pallas as pl
from jax.experimental.pallas import tpu as pltpu

F32 = jnp.float32
BF16 = jnp.bfloat16

LANES = 128
SUBLANES = 8
VMEM_LIMIT = 48 * 1024 * 1024
VMEM_LIMIT_WIDE = 56 * 1024 * 1024

R_HEAD = 64
G_HEAD = 128
G_CONV = 4
R_HB = 8
G_HB = 4
HBW = 512
CHUNK = 64
SEQ_BLOCK_ROWS = 64
R_LONG_SEQ_BLOCK = 4
G_LONG_SEQ_BLOCK = 4
NORM_EPS = 1e-6
RWKV_GN_EPS = 64e-5
PASSES_MAIN = 1
R_PASSES_INV = 3
G_PASSES_INV = 3

NN = ((1,), (0,))
NT = ((1,), (1,))
TN = ((0,), (0,))


def _cparams(sem, vmem=VMEM_LIMIT):
    return pltpu.CompilerParams(dimension_semantics=sem, vmem_limit_bytes=vmem)


def _pick(n, cands):
    for c in cands:
        if n % c == 0:
            return c
    raise ValueError(f"no tile in {cands} divides {n}")


def _dot(a, b, dims=NN):
    return lax.dot_general(a, b, (dims, ((), ())), preferred_element_type=F32)


def _mm1(a, b, dims=NN):
    return _dot(a.astype(BF16), b.astype(BF16), dims)


def _split2(a):
    hi = a.astype(BF16)
    lo = (a - hi.astype(F32)).astype(BF16)
    return hi, lo


def _mm3(a, b, dims=NN):
    ah, al = _split2(a)
    bh, bl = _split2(b)
    if dims == TN:
        return _dot(ah, bh, dims) + (_dot(ah, bl, dims) + _dot(al, bh, dims))
    m = a.shape[0]
    ah32 = ah.astype(F32)
    stacked = jnp.concatenate([ah32, a - ah32], axis=0).astype(BF16)
    r = _dot(stacked, bh, dims)
    return r[:m] + (_dot(ah, bl, dims) + r[m:])


def _mm(a, b, dims=NN, passes=1):
    return _mm1(a, b, dims) if passes == 1 else _mm3(a, b, dims)


def _mm3_presplit(a, b_hl):
    m = a.shape[0]
    kdim = b_hl.shape[0] // 2
    ah = a.astype(BF16)
    ah32 = ah.astype(F32)
    stacked = jnp.concatenate([ah32, a - ah32], axis=0).astype(BF16)
    r = _dot(stacked, b_hl[:kdim])
    return r[:m] + (_dot(ah, b_hl[kdim:]) + r[m:])


def _mm_exact_lhs(a_bf16, b, dims=NN):
    b1 = b.astype(BF16)
    r1 = b - b1.astype(F32)
    b2 = r1.astype(BF16)
    b3 = (r1 - b2.astype(F32)).astype(BF16)
    return _dot(a_bf16, b1, dims) + (_dot(a_bf16, b2, dims) + _dot(a_bf16, b3, dims))


def _sigmoid(x):
    return 1.0 / (1.0 + jnp.exp(-x))


def _silu(x):
    return x * _sigmoid(x)


def _softplus(x):
    return jnp.maximum(x, 0.0) + jnp.log1p(jnp.exp(-jnp.abs(x)))


def _tri_masks(c, reps=1):
    row = lax.broadcasted_iota(jnp.int32, (c, reps * c), 0)
    col = lax.broadcasted_iota(jnp.int32, (c, reps * c), 1) % c
    return row > col, row >= col


def _block_cumsum_matrix(rows, tb):
    row = lax.broadcasted_iota(jnp.int32, (rows, rows), 0)
    col = lax.broadcasted_iota(jnp.int32, (rows, rows), 1)
    keep = (row >= col) & (row // tb == col // tb)
    return jnp.where(keep, 1.0, 0.0).astype(BF16)


INV_PACK = 4


def _inv_unit_lower(ns, levels, passes):
    c = ns[0].shape[0]
    pack = INV_PACK if (c * INV_PACK == 2 * LANES and len(ns) % INV_PACK == 0) else 1
    width = pack * c
    row = lax.broadcasted_iota(jnp.int32, (c, width), 0)
    col = lax.broadcasted_iota(jnp.int32, (c, width), 1)
    eye = jnp.where(row == col % c, 1.0, 0.0).astype(F32)
    blk_of_lane = lax.broadcasted_iota(jnp.int32, (1, width), 1) // c

    def bdiag(w):
        if pack == 1:
            return w
        zero = jnp.zeros_like(w)
        return jnp.concatenate([jnp.where(blk_of_lane == i, w, zero) for i in range(pack)], axis=0)

    def mul(lhs, p):
        if passes == 1:
            return _dot(lhs.astype(BF16), bdiag(p.astype(BF16)))
        m = lhs.shape[0]
        ph = p.astype(BF16)
        pl_ = (p - ph.astype(F32)).astype(BF16)
        lh = lhs.astype(BF16)
        lh32 = lh.astype(F32)
        stacked = jnp.concatenate([lh32, lhs - lh32], axis=0).astype(BF16)
        r = _dot(stacked, bdiag(ph))
        return r[:m] + (_dot(lh, bdiag(pl_)) + r[m:])

    packed = [ns[q] if pack == 1 else jnp.concatenate(ns[q:q + pack], axis=1)
              for q in range(0, len(ns), pack)]
    ts = [eye + n for n in packed]
    if levels > 1:
        ps = [mul(n, n) for n in packed]
        for lvl in range(1, levels):
            if lvl == levels - 1:
                ts = [t + mul(t, p) for t, p in zip(ts, ps)]
            else:
                rs = [mul(jnp.concatenate([t, p], axis=0), p) for t, p in zip(ts, ps)]
                ts = [t + r[:c] for t, r in zip(ts, rs)]
                ps = [r[c:] for r in rs]
    if pack == 1:
        return ts
    return [t[:, i * c:(i + 1) * c] for t in ts for i in range(pack)]


def _shift_rows(x, hists, k, tb):
    rows = x.shape[0]
    xr = pltpu.roll(x, k, 0)
    hr = [pltpu.roll(h, k, 0) for h in hists]
    if tb == SUBLANES:
        hfull = hr[0] if len(hr) == 1 else jnp.concatenate(hr, axis=0)
        row = lax.broadcasted_iota(jnp.int32, (rows, 1), 0)
        return jnp.where(row % tb < k, hfull, xr)
    row8 = lax.broadcasted_iota(jnp.int32, (SUBLANES, 1), 0)
    pieces = []
    for j, h in enumerate(hr):
        pieces.append(jnp.where(row8 < k, h, xr[j * tb:j * tb + SUBLANES]))
        pieces.append(xr[j * tb + SUBLANES:(j + 1) * tb])
    return jnp.concatenate(pieces, axis=0)


def _rows_from(parts, tb):
    tiles = [jnp.broadcast_to(p, (tb, p.shape[1])) for p in parts]
    return tiles[0] if len(tiles) == 1 else jnp.concatenate(tiles, axis=0)


def _adaln_kernel(c_ref, w_ref, b_ref, o_ref):
    c = c_ref[...]
    s = _silu(c)
    o_ref[...] = _mm1(s, w_ref[...]) + b_ref[...]


def _adaln(c_all, w_ada, b_ada):
    m, d = c_all.shape
    n = w_ada.shape[1]
    tn = _pick(n, (512, 256, 128))
    return pl.pallas_call(
        _adaln_kernel,
        grid=(n // tn,),
        in_specs=[pl.BlockSpec((m, d), lambda j: (0, 0)),
                  pl.BlockSpec((d, tn), lambda j: (0, j)),
                  pl.BlockSpec((1, tn), lambda j: (0, j))],
        out_specs=pl.BlockSpec((m, tn), lambda j: (0, j)),
        out_shape=jax.ShapeDtypeStruct((m, n), F32),
        compiler_params=_cparams(("parallel",)),
        name="adaln",
    )(c_all, w_ada, b_ada.reshape(1, n))


class _Mods:
    def __init__(self, arr, d, rows_per_seq):
        self.arr, self.d, self.rows_per_seq = arr, d, rows_per_seq

    def spec(self, tm, tn, which, ij):
        off = which * self.d // tn
        if self.arr.ndim == 3:
            tps = self.rows_per_seq // tm
            return pl.BlockSpec((1, 1, tn), lambda *g: (ij(*g)[0] // tps, 0, off + ij(*g)[1]))
        return pl.BlockSpec((tm // self.rows_per_seq, tn), lambda *g: (ij(*g)[0], off + ij(*g)[1]))


def _mod_val(ref, rows):
    v = ref[...]
    if v.ndim == 3:
        return v[0]
    seqs = v.shape[0]
    r = lax.broadcasted_iota(jnp.int32, (rows, seqs), 0) // (rows // seqs)
    c = lax.broadcasted_iota(jnp.int32, (rows, seqs), 1)
    return _mm_exact_lhs(jnp.where(r == c, 1.0, 0.0).astype(BF16), v)


def _norm_mod_kernel(x_ref, w_ref, sc_ref, sh_ref, o_ref):
    x = x_ref[...]
    y = x * lax.rsqrt(jnp.mean(x * x, axis=-1, keepdims=True) + NORM_EPS) * w_ref[...]
    rows = x.shape[0]
    o_ref[...] = (y * (1.0 + _mod_val(sc_ref, rows)) + _mod_val(sh_ref, rows)).astype(o_ref.dtype)


def _norm_mod(x, w, mods, sh_idx, sc_idx, tm):
    m, d = x.shape
    ij = lambda i: (i, 0)
    return pl.pallas_call(
        _norm_mod_kernel,
        grid=(m // tm,),
        in_specs=[pl.BlockSpec((tm, d), lambda i: (i, 0)),
                  pl.BlockSpec((1, d), lambda i: (0, 0)),
                  mods.spec(tm, d, sc_idx, ij),
                  mods.spec(tm, d, sh_idx, ij)],
        out_specs=pl.BlockSpec((tm, d), lambda i: (i, 0)),
        out_shape=jax.ShapeDtypeStruct((m, d), BF16),
        compiler_params=_cparams(("parallel",)),
        name="norm_mod",
    )(x, w.reshape(1, d), mods.arr, mods.arr)


def _mm_kernel(a_ref, w_ref, o_ref):
    o_ref[...] = _dot(a_ref[...], w_ref[...]).astype(o_ref.dtype)


def _matmul(a, w, tm, tn, out_dtype, name):
    m, k = a.shape
    n = w.shape[1]
    return pl.pallas_call(
        _mm_kernel,
        grid=(n // tn, m // tm),
        in_specs=[pl.BlockSpec((tm, k), lambda j, i: (i, 0)),
                  pl.BlockSpec((k, tn), lambda j, i: (0, j))],
        out_specs=pl.BlockSpec((tm, tn), lambda j, i: (i, j)),
        out_shape=jax.ShapeDtypeStruct((m, n), out_dtype),
        compiler_params=_cparams(("parallel", "parallel")),
        name=name,
    )(a, w)


def _relu2_kernel(a_ref, w_ref, o_ref):
    u = jnp.maximum(_dot(a_ref[...], w_ref[...]), 0.0)
    o_ref[...] = (u * u).astype(o_ref.dtype)


def _up_proj(a, w, tm, tn):
    m, k = a.shape
    n = w.shape[1]
    return pl.pallas_call(
        _relu2_kernel,
        grid=(n // tn, m // tm),
        in_specs=[pl.BlockSpec((tm, k), lambda j, i: (i, 0)),
                  pl.BlockSpec((k, tn), lambda j, i: (0, j))],
        out_specs=pl.BlockSpec((tm, tn), lambda j, i: (i, j)),
        out_shape=jax.ShapeDtypeStruct((m, n), BF16),
        compiler_params=_cparams(("parallel", "parallel")),
        name="up_proj",
    )(a, w)


def _merge_kernel(ya_ref, yb_ref, wa_ref, wb_ref, ga_ref, gb_ref, o_ref):
    pa = _dot(ya_ref[...], wa_ref[...])
    pb = _dot(yb_ref[...], wb_ref[...])
    o_ref[...] = (_sigmoid(ga_ref[...]) * pa + _sigmoid(gb_ref[...]) * pb).astype(o_ref.dtype)


def _merge(ya, yb, wa, wb, p_all, moff, tm, tn):
    m, ka = ya.shape
    kb = yb.shape[1]
    d = wa.shape[1]
    oa, ob = moff // tn, (moff + d) // tn
    return pl.pallas_call(
        _merge_kernel,
        grid=(d // tn, m // tm),
        in_specs=[pl.BlockSpec((tm, ka), lambda j, i: (i, 0)),
                  pl.BlockSpec((tm, kb), lambda j, i: (i, 0)),
                  pl.BlockSpec((ka, tn), lambda j, i: (0, j)),
                  pl.BlockSpec((kb, tn), lambda j, i: (0, j)),
                  pl.BlockSpec((tm, tn), lambda j, i: (i, oa + j)),
                  pl.BlockSpec((tm, tn), lambda j, i: (i, ob + j))],
        out_specs=pl.BlockSpec((tm, tn), lambda j, i: (i, j)),
        out_shape=jax.ShapeDtypeStruct((m, d), BF16),
        compiler_params=_cparams(("parallel", "parallel")),
        name="merge",
    )(ya, yb, wa, wb, p_all, p_all)


def _resid_kernel(a_ref, w_ref, x_ref, g_ref, o_ref):
    o_ref[...] = x_ref[...] + _mod_val(g_ref, x_ref.shape[0]) * _dot(a_ref[...], w_ref[...])


def _out_resid(a, w, x, mods, gate_idx, tm, tn):
    m, k = a.shape
    n = w.shape[1]
    ij = lambda j, i: (i, j)
    return pl.pallas_call(
        _resid_kernel,
        grid=(n // tn, m // tm),
        in_specs=[pl.BlockSpec((tm, k), lambda j, i: (i, 0)),
                  pl.BlockSpec((k, tn), lambda j, i: (0, j)),
                  pl.BlockSpec((tm, tn), lambda j, i: (i, j)),
                  mods.spec(tm, tn, gate_idx, ij)],
        out_specs=pl.BlockSpec((tm, tn), lambda j, i: (i, j)),
        out_shape=jax.ShapeDtypeStruct((m, n), F32),
        compiler_params=_cparams(("parallel", "parallel")),
        name="out_resid",
    )(a, w, x, mods.arr)


def _down_kernel(a_ref, w_ref, x_ref, g_ref, fw_ref, o_ref):
    kk = pl.program_id(1)

    @pl.when(kk == 0)
    def _():
        o_ref[...] = jnp.zeros_like(o_ref)

    o_ref[...] = _dot(a_ref[...], w_ref[...]) + o_ref[...]

    @pl.when(kk == pl.num_programs(1) - 1)
    def _():
        x2 = x_ref[...] + _mod_val(g_ref, o_ref.shape[0]) * o_ref[...]
        y = x2 * lax.rsqrt(jnp.mean(x2 * x2, axis=-1, keepdims=True) + NORM_EPS)
        o_ref[...] = y * fw_ref[...]


def _down_final(a, w, x, mods, gate_idx, final_w, tm, tk):
    m, k = a.shape
    n = w.shape[1]
    ij = lambda i, kk: (i, 0)
    return pl.pallas_call(
        _down_kernel,
        grid=(m // tm, k // tk),
        in_specs=[pl.BlockSpec((tm, tk), lambda i, kk: (i, kk)),
                  pl.BlockSpec((tk, n), lambda i, kk: (kk, 0)),
                  pl.BlockSpec((tm, n), lambda i, kk: (i, 0), pipeline_mode=pl.Buffered(1)),
                  mods.spec(tm, n, gate_idx, ij),
                  pl.BlockSpec((1, n), lambda i, kk: (0, 0))],
        out_specs=pl.BlockSpec((tm, n), lambda i, kk: (i, 0), pipeline_mode=pl.Buffered(1)),
        out_shape=jax.ShapeDtypeStruct((m, n), F32),
        compiler_params=_cparams(("parallel", "arbitrary"), VMEM_LIMIT_WIDE),
        name="down_final",
    )(a, w, x, mods.arr, final_w.reshape(1, n))


def _rwkv_kernel(*refs, tb, sb, t_valid, has_state, levels):
    (r_ref, k_ref, v_ref, lo_ref, mur_ref, muk_ref, muv_ref, mul_ref,
     w0_ref, a0_ref, kk_ref, ka_ref, rk_ref, lw_ref, lb_ref,
     ww2_ref, wa2_ref, wg2_ref) = refs[:18]
    n = 18
    if has_state:
        s0_ref, shr_ref, shk_ref, shv_ref, shl_ref = refs[n:n + 5]
        n += 5
    ya_ref, sout_ref, s_ref, prev_ref = refs[n:n + 4]
    ci = pl.program_id(2)
    rows = sb * tb
    nh = HBW // R_HEAD

    @pl.when(ci == 0)
    def _():
        if has_state:
            s_ref[...] = s0_ref[...]
            for slot, sh in enumerate((shr_ref, shk_ref, shv_ref, shl_ref)):
                for j in range(sb):
                    prev_ref[slot, j, 0:1, :] = sh[j]
        else:
            s_ref[...] = jnp.zeros_like(s_ref)
            prev_ref[...] = jnp.zeros_like(prev_ref)

    row = lax.broadcasted_iota(jnp.int32, (rows, 1), 0)
    is_first = row % tb == 0

    def tshift(x_ref, slot, mu_ref):
        x = x_ref[...].reshape(rows, HBW)
        first = _rows_from([prev_ref[slot, j, 0:1, :] for j in range(sb)], tb)
        prev = jnp.where(is_first, first, pltpu.roll(x, 1, 0))
        for j in range(sb):
            prev_ref[slot, j, 0:1, :] = x[(j + 1) * tb - 1:(j + 1) * tb, :]
        return x + (prev - x) * mu_ref[...]

    r = tshift(r_ref, 0, mur_ref)
    k = tshift(k_ref, 1, muk_ref)
    v = tshift(v_ref, 2, muv_ref)
    lo = tshift(lo_ref, 3, mul_ref)
    dlw = ww2_ref.shape[0] // 2
    dla = wa2_ref.shape[0] // 2
    dlg = wg2_ref.shape[0] // 2
    dw = lo[:, 0:dlw]
    da = lo[:, dlw:dlw + dla]
    dg = lo[:, dlw + dla:dlw + dla + dlg]

    log_w = -_softplus(-(w0_ref[...] + _mm3_presplit(jnp.tanh(dw), ww2_ref[...]))) - 0.5
    lw = -jnp.exp(log_w)
    a = _sigmoid(a0_ref[...] + _mm3_presplit(da, wa2_ref[...]))
    g = _mm3_presplit(_sigmoid(dg), wg2_ref[...])

    if t_valid is not None:
        valid = (row % tb < t_valid).astype(F32)
        lw = lw * valid
        k = k * valid
        v = v * valid

    cum = _mm_exact_lhs(_block_cumsum_matrix(rows, tb), lw)
    w_inc = jnp.exp(cum)
    w_inv = jnp.exp(-cum)
    w_exc = jnp.exp(cum - lw)
    rp = r * w_inc
    kx = k * kk_ref[...]
    k2 = k * (1.0 + (a - 1.0) * ka_ref[...])
    kp = k2 * w_inv
    rk2 = r * k2 * rk_ref[...]
    aw_inv = a * w_inv

    strict2, causal2 = _tri_masks(tb, 2)
    units = [(j, h) for j in range(sb) for h in range(nh)]

    def blk(x, j, h):
        return x[j * tb:(j + 1) * tb, h * R_HEAD:(h + 1) * R_HEAD]

    xs_, ys_, vs_ = [], [], []
    for j, h in units:
        kx_u = blk(kx, j, h)
        kk_u = kx_u * lax.rsqrt(jnp.sum(kx_u * kx_u, axis=-1, keepdims=True) + 1e-6)
        ap = -kk_u * blk(w_exc, j, h)
        bp = kk_u * blk(aw_inv, j, h)
        xs_.append(jnp.concatenate([ap, blk(rp, j, h)], axis=0))
        ys_.append(jnp.concatenate([bp, blk(kp, j, h)], axis=0))
        vs_.append(blk(v, j, h))
    s_old = [s_ref[j, h] for j, h in units]
    amats = [_mm(x, y, NT, PASSES_MAIN) for x, y in zip(xs_, ys_)]
    xss = [_mm(x, s, NT, PASSES_MAIN) for x, s in zip(xs_, s_old)]
    tops = [jnp.where(strict2, am[:tb], 0.0) for am in amats]
    bots = [jnp.where(causal2, am[tb:], 0.0) for am in amats]
    tinvs = _inv_unit_lower([tp[:, :tb] for tp in tops], levels, R_PASSES_INV)
    rhs = [xs[:tb] + _mm(tp[:, tb:], vv, NN, PASSES_MAIN) for xs, tp, vv in zip(xss, tops, vs_)]
    us = [_mm(t, rr, NN, PASSES_MAIN) for t, rr in zip(tinvs, rhs)]
    uvs = [jnp.concatenate([u, vv], axis=0) for u, vv in zip(us, vs_)]
    os_ = [xs[tb:] + _mm(bt, uv, NN, PASSES_MAIN) for xs, bt, uv in zip(xss, bots, uvs)]
    upd = [_mm(uv, y, TN, PASSES_MAIN) for uv, y in zip(uvs, ys_)]
    for i, (j, h) in enumerate(units):
        s_ref[j, h] = (s_old[i] + upd[i]) * blk(w_inc, j, h)[tb - 1:tb, :]

    lnw = lw_ref[...]
    lnb = lb_ref[...]
    out_rows = []
    for j in range(sb):
        outs = []
        for h in range(nh):
            o = os_[j * nh + h]
            mu = jnp.mean(o, axis=-1, keepdims=True)
            var = jnp.mean(jnp.square(o - mu), axis=-1, keepdims=True)
            sl = slice(h * R_HEAD, (h + 1) * R_HEAD)
            on = (o - mu) * lax.rsqrt(var + RWKV_GN_EPS) * lnw[:, sl] + lnb[:, sl]
            bonus = jnp.sum(blk(rk2, j, h), axis=-1, keepdims=True) * vs_[j * nh + h]
            outs.append(on + bonus)
        out_rows.append(jnp.concatenate(outs, axis=1))
    out = out_rows[0] if sb == 1 else jnp.concatenate(out_rows, axis=0)
    ya_ref[...] = (out * g).reshape(sb, tb, HBW).astype(ya_ref.dtype)

    @pl.when(ci == pl.num_programs(2) - 1)
    def _():
        sout_ref[...] = s_ref[...]


def _seq_block(nb, t_pad, nc, long_seq_block):
    sb = long_seq_block if nc > 1 else max(1, SEQ_BLOCK_ROWS // t_pad)
    while nb % sb:
        sb -= 1
    return sb


def _rwkv(p_all, nb, t_pad, t_valid, rw, prm, state):
    tb = min(CHUNK, t_pad)
    nc = t_pad // tb
    sb = _seq_block(nb, t_pad, nc, R_LONG_SEQ_BLOCK)
    nhb = rw // HBW
    heads = rw // R_HEAD
    kb, vb, lb = rw // HBW, 2 * rw // HBW, 3 * rw // HBW
    has_state = state is not None
    levels = max(1, (min(tb, t_valid or tb) - 1).bit_length())
    ya_dtype = BF16 if tb % (2 * SUBLANES) == 0 else F32

    def pcol(off):
        return pl.BlockSpec((sb, tb, HBW), lambda b, h, ci: (b, ci, off + h))

    def vec(off):
        return pl.BlockSpec((1, HBW), lambda b, h, ci: (0, off + h))

    in_specs = [pcol(0), pcol(kb), pcol(vb),
                pl.BlockSpec((sb, tb, HBW), lambda b, h, ci: (b, ci, lb)),
                vec(0), vec(kb), vec(vb),
                pl.BlockSpec((1, HBW), lambda b, h, ci: (0, lb)),
                vec(0), vec(0), vec(0), vec(0), vec(0), vec(0), vec(0),
                pl.BlockSpec((prm["ww2"].shape[0], HBW), lambda b, h, ci: (0, h)),
                pl.BlockSpec((prm["wa2"].shape[0], HBW), lambda b, h, ci: (0, h)),
                pl.BlockSpec((prm["wg2"].shape[0], HBW), lambda b, h, ci: (0, h))]
    args = [p_all, p_all, p_all, p_all, prm["mu"], prm["mu"], prm["mu"], prm["mu"],
            prm["w0"], prm["a0"], prm["k_k"], prm["k_a"], prm["r_k"], prm["lnx_w"], prm["lnx_b"],
            prm["ww2"], prm["wa2"], prm["wg2"]]
    state_spec = pl.BlockSpec((sb, R_HB, R_HEAD, R_HEAD), lambda b, h, ci: (b, h, 0, 0))
    if has_state:
        s0, shift = state
        in_specs += [state_spec,
                     pl.BlockSpec((sb, 1, HBW), lambda b, h, ci: (b, 0, h)),
                     pl.BlockSpec((sb, 1, HBW), lambda b, h, ci: (b, 0, kb + h)),
                     pl.BlockSpec((sb, 1, HBW), lambda b, h, ci: (b, 0, vb + h)),
                     pl.BlockSpec((sb, 1, HBW), lambda b, h, ci: (b, 0, lb))]
        args += [s0, shift, shift, shift, shift]
    kern = functools.partial(_rwkv_kernel, tb=tb, sb=sb, t_valid=t_valid, has_state=has_state,
                             levels=levels)
    return pl.pallas_call(
        kern,
        grid=(nb // sb, nhb, nc),
        in_specs=in_specs,
        out_specs=[pl.BlockSpec((sb, tb, HBW), lambda b, h, ci: (b, ci, h)), state_spec],
        out_shape=[jax.ShapeDtypeStruct((nb, t_pad, rw), ya_dtype),
                   jax.ShapeDtypeStruct((nb, heads, R_HEAD, R_HEAD), F32)],
        scratch_shapes=[pltpu.VMEM((sb, R_HB, R_HEAD, R_HEAD), F32),
                        pltpu.VMEM((4, sb, SUBLANES, HBW), F32)],
        compiler_params=_cparams(("parallel", "parallel", "arbitrary")),
        name="rwkv7_chunk",
    )(*args)


def _gdn_kernel(*refs, tb, sb, t_valid, has_state, levels):
    (q_ref, k_ref, v_ref, z_ref, ba_ref, cq_ref, ck_ref, cv_ref,
     an_ref, dt_ref, nw_ref) = refs[:11]
    n = 11
    if has_state:
        s0_ref, hq_ref, hk_ref, hv_ref = refs[n:n + 4]
        n += 4
    yb_ref, sout_ref, s_ref, hist_ref = refs[n:n + 4]
    ci = pl.program_id(2)
    rows = sb * tb

    @pl.when(ci == 0)
    def _():
        if has_state:
            s_ref[...] = s0_ref[...]
            hist_ref[0] = hq_ref[...]
            hist_ref[1] = hk_ref[...]
            hist_ref[2] = hv_ref[...]
        else:
            s_ref[...] = jnp.zeros_like(s_ref)
            hist_ref[...] = jnp.zeros_like(hist_ref)

    row = lax.broadcasted_iota(jnp.int32, (rows, 1), 0)

    def conv(x_ref, slot, cw_ref):
        x = x_ref[...].reshape(rows, HBW)
        hists = [hist_ref[slot, j] for j in range(sb)]
        cw = cw_ref[...]
        acc = _shift_rows(x, hists, 3, tb) * cw[0:1, :]
        acc = acc + _shift_rows(x, hists, 2, tb) * cw[1:2, :]
        acc = acc + _shift_rows(x, hists, 1, tb) * cw[2:3, :]
        acc = acc + x * cw[3:4, :]
        for j in range(sb):
            hist_ref[slot, j] = x[(j + 1) * tb - SUBLANES:(j + 1) * tb, :]
        return _silu(acc)

    q = conv(q_ref, 0, cq_ref)
    k = conv(k_ref, 1, ck_ref)
    v = conv(v_ref, 2, cv_ref)
    z = z_ref[...].reshape(rows, HBW)
    ba = ba_ref[...].reshape(rows, LANES)
    beta_all = _sigmoid(ba)
    la_all = an_ref[...] * _softplus(ba + dt_ref[...])
    if t_valid is not None:
        valid = (row % tb < t_valid).astype(F32)
        beta_all = beta_all * valid
        la_all = la_all * valid
        q = q * valid
        k = k * valid
        v = v * valid

    g_all = _mm_exact_lhs(_block_cumsum_matrix(rows, tb), la_all)
    lrow = lax.broadcasted_iota(jnp.int32, (LANES, LANES), 0)
    lcol = lax.broadcasted_iota(jnp.int32, (LANES, LANES), 1)
    eye = jnp.where(lrow == lcol, 1.0, 0.0).astype(BF16)
    g_t = _mm_exact_lhs(eye, g_all, NT)
    eg_all = jnp.exp(g_all)
    strict, causal = _tri_masks(tb)
    nw = nw_ref[...]
    nh = q.shape[1] // G_HEAD
    units = [(j, h) for j in range(sb) for h in range(nh)]

    def blk(x, j, h):
        return x[j * tb:(j + 1) * tb, h * G_HEAD:(h + 1) * G_HEAD]

    qs, ks, kbs, rhs, decays, egs, gcols = [], [], [], [], [], [], []
    for j, h in units:
        rs = slice(j * tb, (j + 1) * tb)
        gcol = g_all[rs, G_HB + h:G_HB + h + 1]
        grow = g_t[G_HB + h:G_HB + h + 1, rs]
        beta = beta_all[rs, h:h + 1]
        eg = eg_all[rs, G_HB + h:G_HB + h + 1]
        decays.append(jnp.where(causal, jnp.exp(jnp.where(causal, gcol - grow, 0.0)), 0.0))
        q_u, k_u, v_u = blk(q, j, h), blk(k, j, h), blk(v, j, h)
        q_u = q_u * lax.rsqrt(jnp.sum(q_u * q_u, axis=-1, keepdims=True) + 1e-6) * (G_HEAD ** -0.5)
        k_u = k_u * lax.rsqrt(jnp.sum(k_u * k_u, axis=-1, keepdims=True) + 1e-6)
        kb = k_u * beta
        qs.append(q_u)
        ks.append(k_u)
        kbs.append(kb)
        rhs.append(v_u * beta)
        egs.append(eg)
        gcols.append(gcol)
    s_old = [s_ref[j, h] for j, h in units]
    amats = [_mm(jnp.concatenate([kb, q_u], axis=0), k_u, NT, G_PASSES_INV)
             for kb, q_u, k_u in zip(kbs, qs, ks)]
    lmats = [jnp.where(strict, am[:tb] * dc, 0.0) for am, dc in zip(amats, decays)]
    attns = [am[tb:] * dc for am, dc in zip(amats, decays)]
    tinvs = _inv_unit_lower([-lm for lm in lmats], levels, G_PASSES_INV)
    xss = [_mm(jnp.concatenate([kb * eg, q_u * eg], axis=0), s, NN, PASSES_MAIN)
           for kb, q_u, eg, s in zip(kbs, qs, egs, s_old)]
    vnews = [_mm(t, rr - xs[:tb], NN, G_PASSES_INV) for t, rr, xs in zip(tinvs, rhs, xss)]
    os_ = [xs[tb:] + _mm(at, vn, NN, PASSES_MAIN) for xs, at, vn in zip(xss, attns, vnews)]
    ktails = [k_u * jnp.exp(gc[tb - 1:tb, :] - gc) for k_u, gc in zip(ks, gcols)]
    upd = [_mm(kt, vn, TN, PASSES_MAIN) for kt, vn in zip(ktails, vnews)]

    out_rows = []
    for j in range(sb):
        outs = []
        for h in range(nh):
            i = j * nh + h
            s_ref[j, h] = s_old[i] * egs[i][tb - 1:tb, :] + upd[i]
            o = os_[i]
            o = o * lax.rsqrt(jnp.mean(o * o, axis=-1, keepdims=True) + NORM_EPS) * nw
            outs.append(o)
        out_rows.append(jnp.concatenate(outs, axis=1))
    out = out_rows[0] if sb == 1 else jnp.concatenate(out_rows, axis=0)
    yb_ref[...] = (out * _silu(z)).reshape(sb, tb, HBW).astype(yb_ref.dtype)

    @pl.when(ci == pl.num_programs(2) - 1)
    def _():
        sout_ref[...] = s_ref[...]


def _gdn(p_all, nb, t_pad, t_valid, gw, goff, baoff, prm, state):
    tb = min(CHUNK, t_pad)
    nc = t_pad // tb
    sb = _seq_block(nb, t_pad, nc, G_LONG_SEQ_BLOCK)
    nhb = gw // HBW
    heads = gw // G_HEAD
    qb = goff // HBW
    kb, vb, zb = qb + gw // HBW, qb + 2 * gw // HBW, qb + 3 * gw // HBW
    bab = baoff // LANES
    has_state = state is not None
    levels = max(1, (min(tb, t_valid or tb) - 1).bit_length())
    yb_dtype = BF16 if tb % (2 * SUBLANES) == 0 else F32

    def pcol(off):
        return pl.BlockSpec((sb, tb, HBW), lambda b, h, ci: (b, ci, off + h))

    def cwspec(off):
        return pl.BlockSpec((G_CONV, HBW), lambda b, h, ci: (0, off + h))

    in_specs = [pcol(qb), pcol(kb), pcol(vb), pcol(zb),
                pl.BlockSpec((sb, tb, LANES), lambda b, h, ci: (b, ci, bab + h)),
                cwspec(0), cwspec(gw // HBW), cwspec(2 * gw // HBW),
                pl.BlockSpec((1, LANES), lambda b, h, ci: (0, h)),
                pl.BlockSpec((1, LANES), lambda b, h, ci: (0, h)),
                pl.BlockSpec((1, G_HEAD), lambda b, h, ci: (0, 0))]
    args = [p_all, p_all, p_all, p_all, p_all, prm["conv_w"], prm["conv_w"], prm["conv_w"],
            prm["neg_a"], prm["dt"], prm["norm_w"]]
    state_spec = pl.BlockSpec((sb, G_HB, G_HEAD, G_HEAD), lambda b, h, ci: (b, h, 0, 0))
    if has_state:
        s0, hist = state
        in_specs += [state_spec,
                     pl.BlockSpec((sb, SUBLANES, HBW), lambda b, h, ci: (b, 0, h)),
                     pl.BlockSpec((sb, SUBLANES, HBW), lambda b, h, ci: (b, 0, gw // HBW + h)),
                     pl.BlockSpec((sb, SUBLANES, HBW), lambda b, h, ci: (b, 0, 2 * gw // HBW + h))]
        args += [s0, hist, hist, hist]
    kern = functools.partial(_gdn_kernel, tb=tb, sb=sb, t_valid=t_valid, has_state=has_state,
                             levels=levels)
    return pl.pallas_call(
        kern,
        grid=(nb // sb, nhb, nc),
        in_specs=in_specs,
        out_specs=[pl.BlockSpec((sb, tb, HBW), lambda b, h, ci: (b, ci, h)), state_spec],
        out_shape=[jax.ShapeDtypeStruct((nb, t_pad, gw), yb_dtype),
                   jax.ShapeDtypeStruct((nb, heads, G_HEAD, G_HEAD), F32)],
        scratch_shapes=[pltpu.VMEM((sb, G_HB, G_HEAD, G_HEAD), F32),
                        pltpu.VMEM((3, sb, SUBLANES, HBW), F32)],
        compiler_params=_cparams(("parallel", "parallel", "arbitrary")),
        name="gdn_chunk",
    )(*args)


def _run_group(x, mods, nb, t, lay, wts, rprm, gprm, states):
    m, d = x.shape
    rw, gw = lay["rw"], lay["gw"]
    row_tiles = (1024, 512, 256, 128, 64, 32, 16, 8)
    span = t if mods.arr.ndim == 3 else m
    tm = _pick(span, row_tiles[1:])
    tm_proj = _pick(span, row_tiles)
    n_all = wts["w_all"].shape[1]
    tm_norm = min(tm, 256)

    h = _norm_mod(x, wts["norm1_w"], mods, 0, 1, tm_norm)
    p_all = _matmul(h, wts["w_all"], tm_proj, _pick(n_all, (1024, 512, 256, 128)), F32, "in_proj")

    p3 = p_all.reshape(nb, t, n_all)
    shift_new = p3[:, t - 1, :lay["r_cols"]]
    conv_new = p3[:, t - (G_CONV - 1):, lay["goff"]:lay["goff"] + 3 * gw]

    if t % SUBLANES == 0:
        t_pad, t_valid, p_rec = t, None, p3
    else:
        t_pad = -(-t // SUBLANES) * SUBLANES
        t_valid = t
        p_rec = jnp.pad(p3, ((0, 0), (0, t_pad - t), (0, 0)))

    ya, wkv_new = _rwkv(p_rec, nb, t_pad, t_valid, rw, rprm, states and states["rwkv"])
    yb, gdn_new = _gdn(p_rec, nb, t_pad, t_valid, gw, lay["goff"], lay["baoff"], gprm,
                       states and states["gdn"])
    ya = ya[:, :t].reshape(m, rw).astype(BF16)
    yb = yb[:, :t].reshape(m, gw).astype(BF16)

    tn_m = _pick(d, (1024, 512, 256, 128))
    while lay["moff"] % tn_m:
        tn_m //= 2
    merged = _merge(ya, yb, wts["w_out_a"], wts["w_out_b"], p_all, lay["moff"], tm, tn_m)
    x1 = _out_resid(merged, wts["w_out"], x, mods, 2, tm, _pick(d, (1024, 512, 256, 128)))
    h2 = _norm_mod(x1, wts["norm2_w"], mods, 3, 4, tm_norm)
    dff = wts["w_up"].shape[1]
    up = _up_proj(h2, wts["w_up"], tm_proj, _pick(dff, (1024, 512, 256, 128)))
    y = _down_final(up, wts["w_down"], x1, mods, 5, wts["final_norm_w"], tm,
                    _pick(dff, (1024, 512, 256, 128)))
    return y, wkv_new, shift_new, gdn_new, conv_new


def kernel(x_prompt, x_sample, state_rwkv_wkv, state_rwkv_shift, state_gdn, state_gdn_conv,
           c_prompt, c_sample, norm1_w, norm2_w, w_ada, b_ada, w_in, r_mu, r_w0, r_w_w2, r_a0,
           r_w_a2, r_w_g2, r_k_k, r_k_a, r_r_k, r_lnx_w, r_lnx_b, g_conv_w, g_a_log, g_dt_bias,
           g_norm_w, w_out_a, w_out_b, w_out, w_up, w_down, final_norm_w):
    depth = w_in.shape[0]
    assert depth == 1, "single-layer trunk"
    bp, tp, d = x_prompt.shape
    bs, ts, _ = x_sample.shape
    rw = r_w0.shape[1]
    gw = g_conv_w.shape[2] // 3
    g_heads = g_a_log.shape[1]
    dlw, dla, dlg = r_w_w2.shape[1], r_w_a2.shape[1], r_w_g2.shape[1]
    lora = dlw + dla + dlg
    r_cols = 3 * rw + lora
    g_cols = 4 * gw + 2 * g_heads
    assert rw % HBW == 0 and gw % HBW == 0 and lora <= HBW and g_heads % G_HB == 0
    assert w_in.shape[2] == r_cols + g_cols + 2 * d

    rseg = 3 * rw + HBW
    goff = rseg
    baoff = goff + 4 * gw
    n_ba = (g_heads // G_HB) * LANES
    moff = baoff + n_ba
    lay = dict(rw=rw, gw=gw, r_cols=r_cols, goff=goff, baoff=baoff, moff=moff)

    wi = w_in[0]
    w_b = wi[:, r_cols + 4 * gw:r_cols + 4 * gw + g_heads].reshape(d, g_heads // G_HB, G_HB)
    w_a = wi[:, r_cols + 4 * gw + g_heads:r_cols + g_cols].reshape(d, g_heads // G_HB, G_HB)
    w_ba = jnp.concatenate([w_b, w_a, jnp.zeros((d, g_heads // G_HB, LANES - 2 * G_HB), F32)],
                           axis=2).reshape(d, n_ba)
    w_all = jnp.concatenate([wi[:, :r_cols], jnp.zeros((d, rseg - r_cols), F32),
                             wi[:, r_cols:r_cols + 4 * gw], w_ba,
                             wi[:, r_cols + g_cols:]], axis=1).astype(BF16)

    wts = dict(w_all=w_all, norm1_w=norm1_w[0], norm2_w=norm2_w[0],
               w_out_a=w_out_a[0].astype(BF16), w_out_b=w_out_b[0].astype(BF16),
               w_out=w_out[0].astype(BF16), w_up=w_up[0].astype(BF16),
               w_down=w_down[0].astype(BF16), final_norm_w=final_norm_w)

    row = lambda a: a.reshape(1, -1)

    def hilo(w):
        hi = w.astype(BF16)
        return jnp.concatenate([hi, (w - hi.astype(F32)).astype(BF16)], axis=0)

    rprm = dict(mu=jnp.pad(row(r_mu[0]), ((0, 0), (0, rseg - r_cols))),
                w0=row(r_w0[0]), a0=row(r_a0[0]), k_k=row(r_k_k[0]), k_a=row(r_k_a[0]),
                r_k=row(r_r_k[0]), lnx_w=row(r_lnx_w[0]), lnx_b=row(r_lnx_b[0]),
                ww2=hilo(r_w_w2[0]), wa2=hilo(r_w_a2[0]), wg2=hilo(r_w_g2[0]))

    def ba_row(vals):
        v = vals.reshape(g_heads // G_HB, G_HB)
        z = jnp.zeros_like(v)
        return jnp.concatenate([z, v, jnp.zeros((g_heads // G_HB, LANES - 2 * G_HB), F32)],
                               axis=1).reshape(1, n_ba)

    gprm = dict(conv_w=g_conv_w[0], neg_a=ba_row(-jnp.exp(g_a_log[0])), dt=ba_row(g_dt_bias[0]),
                norm_w=row(g_norm_w[0]))

    c_all = jnp.concatenate([c_prompt, c_sample], axis=0)
    n_seq = bp + bs
    c_all = jnp.pad(c_all, ((0, -n_seq % SUBLANES), (0, 0)))
    mods_all = _adaln(c_all, w_ada[0], b_ada[0])
    mods_p = _Mods(mods_all[:bp].reshape(bp, 1, 6 * d), d, tp)
    mods_s = _Mods(mods_all[bp:n_seq], d, ts)

    y_p, p_wkv, p_shift, p_gdn, p_conv = _run_group(
        x_prompt.reshape(bp * tp, d), mods_p, bp, tp, lay, wts, rprm, gprm, None)

    shift0 = jnp.pad(state_rwkv_shift[0], ((0, 0), (0, rseg - r_cols))).reshape(bs, 1, rseg)
    hist0 = jnp.pad(state_gdn_conv[0], ((0, 0), (SUBLANES - (G_CONV - 1), 0), (0, 0)))
    states = dict(rwkv=(state_rwkv_wkv[0], shift0), gdn=(state_gdn[0], hist0))
    y_s, s_wkv, s_shift, s_gdn, s_conv = _run_group(
        x_sample.reshape(bs * ts, d), mods_s, bs, ts, lay, wts, rprm, gprm, states)

    return (y_p.reshape(bp, tp, d), y_s.reshape(bs, ts, d),
            p_wkv[None], p_shift[None], p_gdn[None], p_conv[None],
            s_wkv[None], s_shift[None], s_gdn[None], s_conv[None])
```

```python
import functools

import jax
import jax.numpy as jnp
from jax import lax
from jax.experimental import pallas as pl
from jax.experimental.pallas import tpu as pltpu

F32 = jnp.float32
BF16 = jnp.bfloat16

LANES = 128
SUBLANES = 8
VMEM_LIMIT = 48 * 1024 * 1024
VMEM_LIMIT_WIDE = 56 * 1024 * 1024

R_HEAD = 64
G_HEAD = 128
G_CONV = 4
R_HB = 8
G_HB = 4
HBW = 512
CHUNK = 64
SEQ_BLOCK_ROWS = 64
R_LONG_SEQ_BLOCK = 4
G_LONG_SEQ_BLOCK = 4
NORM_EPS = 1e-6
RWKV_GN_EPS = 64e-5
PASSES_MAIN = 1
R_PASSES_INV = 3
G_PASSES_INV = 3

NN = ((1,), (0,))
NT = ((1,), (1,))
TN = ((0,), (0,))


def _cparams(sem, vmem=VMEM_LIMIT):
    return pltpu.CompilerParams(dimension_semantics=sem, vmem_limit_bytes=vmem)


def _pick(n, cands):
    for c in cands:
        if n % c == 0:
            return c
    raise ValueError(f"no tile in {cands} divides {n}")


def _dot(a, b, dims=NN):
    return lax.dot_general(a, b, (dims, ((), ())), preferred_element_type=F32)


def _mm1(a, b, dims=NN):
    return _dot(a.astype(BF16), b.astype(BF16), dims)


def _split2(a):
    hi = a.astype(BF16)
    lo = (a - hi.astype(F32)).astype(BF16)
    return hi, lo


def _mm3(a, b, dims=NN):
    ah, al = _split2(a)
    bh, bl = _split2(b)
    if dims == TN:
        return _dot(ah, bh, dims) + (_dot(ah, bl, dims) + _dot(al, bh, dims))
    m = a.shape[0]
    ah32 = ah.astype(F32)
    stacked = jnp.concatenate([ah32, a - ah32], axis=0).astype(BF16)
    r = _dot(stacked, bh, dims)
    return r[:m] + (_dot(ah, bl, dims) + r[m:])


def _mm(a, b, dims=NN, passes=1):
    return _mm1(a, b, dims) if passes == 1 else _mm3(a, b, dims)


def _mm3_presplit(a, b_hl):
    m = a.shape[0]
    kdim = b_hl.shape[0] // 2
    ah = a.astype(BF16)
    ah32 = ah.astype(F32)
    stacked = jnp.concatenate([ah32, a - ah32], axis=0).astype(BF16)
    r = _dot(stacked, b_hl[:kdim])
    return r[:m] + (_dot(ah, b_hl[kdim:]) + r[m:])


def _mm_exact_lhs(a_bf16, b, dims=NN):
    b1 = b.astype(BF16)
    r1 = b - b1.astype(F32)
    b2 = r1.astype(BF16)
    b3 = (r1 - b2.astype(F32)).astype(BF16)
    return _dot(a_bf16, b1, dims) + (_dot(a_bf16, b2, dims) + _dot(a_bf16, b3, dims))


def _sigmoid(x):
    return 1.0 / (1.0 + jnp.exp(-x))


def _silu(x):
    return x * _sigmoid(x)


def _softplus(x):
    return jnp.maximum(x, 0.0) + jnp.log1p(jnp.exp(-jnp.abs(x)))


def _tri_masks(c, reps=1):
    row = lax.broadcasted_iota(jnp.int32, (c, reps * c), 0)
    col = lax.broadcasted_iota(jnp.int32, (c, reps * c), 1) % c
    return row > col, row >= col


def _block_cumsum_matrix(rows, tb):
    row = lax.broadcasted_iota(jnp.int32, (rows, rows), 0)
    col = lax.broadcasted_iota(jnp.int32, (rows, rows), 1)
    keep = (row >= col) & (row // tb == col // tb)
    return jnp.where(keep, 1.0, 0.0).astype(BF16)


INV_PACK = 4


def _inv_unit_lower(ns, levels, passes):
    c = ns[0].shape[0]
    pack = INV_PACK if (c * INV_PACK == 2 * LANES and len(ns) % INV_PACK == 0) else 1
    width = pack * c
    row = lax.broadcasted_iota(jnp.int32, (c, width), 0)
    col = lax.broadcasted_iota(jnp.int32, (c, width), 1)
    eye = jnp.where(row == col % c, 1.0, 0.0).astype(F32)
    blk_of_lane = lax.broadcasted_iota(jnp.int32, (1, width), 1) // c

    def bdiag(w):
        if pack == 1:
            return w
        zero = jnp.zeros_like(w)
        return jnp.concatenate([jnp.where(blk_of_lane == i, w, zero) for i in range(pack)], axis=0)

    def mul(lhs, p):
        if passes == 1:
            return _dot(lhs.astype(BF16), bdiag(p.astype(BF16)))
        m = lhs.shape[0]
        ph = p.astype(BF16)
        pl_ = (p - ph.astype(F32)).astype(BF16)
        lh = lhs.astype(BF16)
        lh32 = lh.astype(F32)
        stacked = jnp.concatenate([lh32, lhs - lh32], axis=0).astype(BF16)
        r = _dot(stacked, bdiag(ph))
        return r[:m] + (_dot(lh, bdiag(pl_)) + r[m:])

    packed = [ns[q] if pack == 1 else jnp.concatenate(ns[q:q + pack], axis=1)
              for q in range(0, len(ns), pack)]
    ts = [eye + n for n in packed]
    if levels > 1:
        ps = [mul(n, n) for n in packed]
        for lvl in range(1, levels):
            if lvl == levels - 1:
                ts = [t + mul(t, p) for t, p in zip(ts, ps)]
            else:
                rs = [mul(jnp.concatenate([t, p], axis=0), p) for t, p in zip(ts, ps)]
                ts = [t + r[:c] for t, r in zip(ts, rs)]
                ps = [r[c:] for r in rs]
    if pack == 1:
        return ts
    return [t[:, i * c:(i + 1) * c] for t in ts for i in range(pack)]


def _shift_rows(x, hists, k, tb):
    rows = x.shape[0]
    xr = pltpu.roll(x, k, 0)
    hr = [pltpu.roll(h, k, 0) for h in hists]
    if tb == SUBLANES:
        hfull = hr[0] if len(hr) == 1 else jnp.concatenate(hr, axis=0)
        row = lax.broadcasted_iota(jnp.int32, (rows, 1), 0)
        return jnp.where(row % tb < k, hfull, xr)
    row8 = lax.broadcasted_iota(jnp.int32, (SUBLANES, 1), 0)
    pieces = []
    for j, h in enumerate(hr):
        pieces.append(jnp.where(row8 < k, h, xr[j * tb:j * tb + SUBLANES]))
        pieces.append(xr[j * tb + SUBLANES:(j + 1) * tb])
    return jnp.concatenate(pieces, axis=0)


def _rows_from(parts, tb):
    tiles = [jnp.broadcast_to(p, (tb, p.shape[1])) for p in parts]
    return tiles[0] if len(tiles) == 1 else jnp.concatenate(tiles, axis=0)


def _adaln_kernel(c_ref, w_ref, b_ref, o_ref):
    c = c_ref[...]
    s = _silu(c)
    o_ref[...] = _mm1(s, w_ref[...]) + b_ref[...]


def _adaln(c_all, w_ada, b_ada):
    m, d = c_all.shape
    n = w_ada.shape[1]
    tn = _pick(n, (512, 256, 128))
    return pl.pallas_call(
        _adaln_kernel,
        grid=(n // tn,),
        in_specs=[pl.BlockSpec((m, d), lambda j: (0, 0)),
                  pl.BlockSpec((d, tn), lambda j: (0, j)),
                  pl.BlockSpec((1, tn), lambda j: (0, j))],
        out_specs=pl.BlockSpec((m, tn), lambda j: (0, j)),
        out_shape=jax.ShapeDtypeStruct((m, n), F32),
        compiler_params=_cparams(("parallel",)),
        name="adaln",
    )(c_all, w_ada, b_ada.reshape(1, n))


class _Mods:
    def __init__(self, arr, d, rows_per_seq):
        self.arr, self.d, self.rows_per_seq = arr, d, rows_per_seq

    def spec(self, tm, tn, which, ij):
        off = which * self.d // tn
        if self.arr.ndim == 3:
            tps = self.rows_per_seq // tm
            return pl.BlockSpec((1, 1, tn), lambda *g: (ij(*g)[0] // tps, 0, off + ij(*g)[1]))
        return pl.BlockSpec((tm // self.rows_per_seq, tn), lambda *g: (ij(*g)[0], off + ij(*g)[1]))


def _mod_val(ref, rows):
    v = ref[...]
    if v.ndim == 3:
        return v[0]
    seqs = v.shape[0]
    r = lax.broadcasted_iota(jnp.int32, (rows, seqs), 0) // (rows // seqs)
    c = lax.broadcasted_iota(jnp.int32, (rows, seqs), 1)
    return _mm_exact_lhs(jnp.where(r == c, 1.0, 0.0).astype(BF16), v)


def _norm_mod_kernel(x_ref, w_ref, sc_ref, sh_ref, o_ref):
    x = x_ref[...]
    y = x * lax.rsqrt(jnp.mean(x * x, axis=-1, keepdims=True) + NORM_EPS) * w_ref[...]
    rows = x.shape[0]
    o_ref[...] = (y * (1.0 + _mod_val(sc_ref, rows)) + _mod_val(sh_ref, rows)).astype(o_ref.dtype)


def _norm_mod(x, w, mods, sh_idx, sc_idx, tm):
    m, d = x.shape
    ij = lambda i: (i, 0)
    return pl.pallas_call(
        _norm_mod_kernel,
        grid=(m // tm,),
        in_specs=[pl.BlockSpec((tm, d), lambda i: (i, 0)),
                  pl.BlockSpec((1, d), lambda i: (0, 0)),
                  mods.spec(tm, d, sc_idx, ij),
                  mods.spec(tm, d, sh_idx, ij)],
        out_specs=pl.BlockSpec((tm, d), lambda i: (i, 0)),
        out_shape=jax.ShapeDtypeStruct((m, d), BF16),
        compiler_params=_cparams(("parallel",)),
        name="norm_mod",
    )(x, w.reshape(1, d), mods.arr, mods.arr)


def _mm_kernel(a_ref, w_ref, o_ref):
    o_ref[...] = _dot(a_ref[...], w_ref[...]).astype(o_ref.dtype)


def _matmul(a, w, tm, tn, out_dtype, name):
    m, k = a.shape
    n = w.shape[1]
    return pl.pallas_call(
        _mm_kernel,
        grid=(n // tn, m // tm),
        in_specs=[pl.BlockSpec((tm, k), lambda j, i: (i, 0)),
                  pl.BlockSpec((k, tn), lambda j, i: (0, j))],
        out_specs=pl.BlockSpec((tm, tn), lambda j, i: (i, j)),
        out_shape=jax.ShapeDtypeStruct((m, n), out_dtype),
        compiler_params=_cparams(("parallel", "parallel")),
        name=name,
    )(a, w)


def _relu2_kernel(a_ref, w_ref, o_ref):
    u = jnp.maximum(_dot(a_ref[...], w_ref[...]), 0.0)
    o_ref[...] = (u * u).astype(o_ref.dtype)


def _up_proj(a, w, tm, tn):
    m, k = a.shape
    n = w.shape[1]
    return pl.pallas_call(
        _relu2_kernel,
        grid=(n // tn, m // tm),
        in_specs=[pl.BlockSpec((tm, k), lambda j, i: (i, 0)),
                  pl.BlockSpec((k, tn), lambda j, i: (0, j))],
        out_specs=pl.BlockSpec((tm, tn), lambda j, i: (i, j)),
        out_shape=jax.ShapeDtypeStruct((m, n), BF16),
        compiler_params=_cparams(("parallel", "parallel")),
        name="up_proj",
    )(a, w)


def _merge_kernel(ya_ref, yb_ref, wa_ref, wb_ref, ga_ref, gb_ref, o_ref):
    pa = _dot(ya_ref[...], wa_ref[...])
    pb = _dot(yb_ref[...], wb_ref[...])
    o_ref[...] = (_sigmoid(ga_ref[...]) * pa + _sigmoid(gb_ref[...]) * pb).astype(o_ref.dtype)


def _merge(ya, yb, wa, wb, p_all, moff, tm, tn):
    m, ka = ya.shape
    kb = yb.shape[1]
    d = wa.shape[1]
    oa, ob = moff // tn, (moff + d) // tn
    return pl.pallas_call(
        _merge_kernel,
        grid=(d // tn, m // tm),
        in_specs=[pl.BlockSpec((tm, ka), lambda j, i: (i, 0)),
                  pl.BlockSpec((tm, kb), lambda j, i: (i, 0)),
                  pl.BlockSpec((ka, tn), lambda j, i: (0, j)),
                  pl.BlockSpec((kb, tn), lambda j, i: (0, j)),
                  pl.BlockSpec((tm, tn), lambda j, i: (i, oa + j)),
                  pl.BlockSpec((tm, tn), lambda j, i: (i, ob + j))],
        out_specs=pl.BlockSpec((tm, tn), lambda j, i: (i, j)),
        out_shape=jax.ShapeDtypeStruct((m, d), BF16),
        compiler_params=_cparams(("parallel", "parallel")),
        name="merge",
    )(ya, yb, wa, wb, p_all, p_all)


def _resid_kernel(a_ref, w_ref, x_ref, g_ref, o_ref):
    o_ref[...] = x_ref[...] + _mod_val(g_ref, x_ref.shape[0]) * _dot(a_ref[...], w_ref[...])


def _out_resid(a, w, x, mods, gate_idx, tm, tn):
    m, k = a.shape
    n = w.shape[1]
    ij = lambda j, i: (i, j)
    return pl.pallas_call(
        _resid_kernel,
        grid=(n // tn, m // tm),
        in_specs=[pl.BlockSpec((tm, k), lambda j, i: (i, 0)),
                  pl.BlockSpec((k, tn), lambda j, i: (0, j)),
                  pl.BlockSpec((tm, tn), lambda j, i: (i, j)),
                  mods.spec(tm, tn, gate_idx, ij)],
        out_specs=pl.BlockSpec((tm, tn), lambda j, i: (i, j)),
        out_shape=jax.ShapeDtypeStruct((m, n), F32),
        compiler_params=_cparams(("parallel", "parallel")),
        name="out_resid",
    )(a, w, x, mods.arr)


def _down_kernel(a_ref, w_ref, x_ref, g_ref, fw_ref, o_ref):
    kk = pl.program_id(1)

    @pl.when(kk == 0)
    def _():
        o_ref[...] = jnp.zeros_like(o_ref)

    o_ref[...] = _dot(a_ref[...], w_ref[...]) + o_ref[...]

    @pl.when(kk == pl.num_programs(1) - 1)
    def _():
        x2 = x_ref[...] + _mod_val(g_ref, o_ref.shape[0]) * o_ref[...]
        y = x2 * lax.rsqrt(jnp.mean(x2 * x2, axis=-1, keepdims=True) + NORM_EPS)
        o_ref[...] = y * fw_ref[...]


def _down_final(a, w, x, mods, gate_idx, final_w, tm, tk):
    m, k = a.shape
    n = w.shape[1]
    ij = lambda i, kk: (i, 0)
    return pl.pallas_call(
        _down_kernel,
        grid=(m // tm, k // tk),
        in_specs=[pl.BlockSpec((tm, tk), lambda i, kk: (i, kk)),
                  pl.BlockSpec((tk, n), lambda i, kk: (kk, 0)),
                  pl.BlockSpec((tm, n), lambda i, kk: (i, 0)),
                  mods.spec(tm, n, gate_idx, ij),
                  pl.BlockSpec((1, n), lambda i, kk: (0, 0))],
        out_specs=pl.BlockSpec((tm, n), lambda i, kk: (i, 0), pipeline_mode=pl.Buffered(1)),
        out_shape=jax.ShapeDtypeStruct((m, n), F32),
        compiler_params=_cparams(("parallel", "arbitrary"), VMEM_LIMIT_WIDE),
        name="down_final",
    )(a, w, x, mods.arr, final_w.reshape(1, n))


def _rwkv_kernel(*refs, tb, sb, t_valid, has_state, levels):
    (r_ref, k_ref, v_ref, lo_ref, mur_ref, muk_ref, muv_ref, mul_ref,
     w0_ref, a0_ref, kk_ref, ka_ref, rk_ref, lw_ref, lb_ref,
     ww2_ref, wa2_ref, wg2_ref) = refs[:18]
    n = 18
    if has_state:
        s0_ref, shr_ref, shk_ref, shv_ref, shl_ref = refs[n:n + 5]
        n += 5
    ya_ref, sout_ref, s_ref, prev_ref = refs[n:n + 4]
    ci = pl.program_id(2)
    rows = sb * tb
    nh = HBW // R_HEAD

    @pl.when(ci == 0)
    def _():
        if has_state:
            s_ref[...] = s0_ref[...]
            for slot, sh in enumerate((shr_ref, shk_ref, shv_ref, shl_ref)):
                for j in range(sb):
                    prev_ref[slot, j, 0:1, :] = sh[j]
        else:
            s_ref[...] = jnp.zeros_like(s_ref)
            prev_ref[...] = jnp.zeros_like(prev_ref)

    row = lax.broadcasted_iota(jnp.int32, (rows, 1), 0)
    is_first = row % tb == 0

    def tshift(x_ref, slot, mu_ref):
        x = x_ref[...].reshape(rows, HBW)
        first = _rows_from([prev_ref[slot, j, 0:1, :] for j in range(sb)], tb)
        prev = jnp.where(is_first, first, pltpu.roll(x, 1, 0))
        for j in range(sb):
            prev_ref[slot, j, 0:1, :] = x[(j + 1) * tb - 1:(j + 1) * tb, :]
        return x + (prev - x) * mu_ref[...]

    r = tshift(r_ref, 0, mur_ref)
    k = tshift(k_ref, 1, muk_ref)
    v = tshift(v_ref, 2, muv_ref)
    lo = tshift(lo_ref, 3, mul_ref)
    dlw = ww2_ref.shape[0] // 2
    dla = wa2_ref.shape[0] // 2
    dlg = wg2_ref.shape[0] // 2
    dw = lo[:, 0:dlw]
    da = lo[:, dlw:dlw + dla]
    dg = lo[:, dlw + dla:dlw + dla + dlg]

    log_w = -_softplus(-(w0_ref[...] + _mm3_presplit(jnp.tanh(dw), ww2_ref[...]))) - 0.5
    lw = -jnp.exp(log_w)
    a = _sigmoid(a0_ref[...] + _mm3_presplit(da, wa2_ref[...]))
    g = _mm3_presplit(_sigmoid(dg), wg2_ref[...])

    if t_valid is not None:
        valid = (row % tb < t_valid).astype(F32)
        lw = lw * valid
        k = k * valid
        v = v * valid

    cum = _mm_exact_lhs(_block_cumsum_matrix(rows, tb), lw)
    w_inc = jnp.exp(cum)
    w_inv = jnp.exp(-cum)
    w_exc = jnp.exp(cum - lw)
    rp = r * w_inc
    kx = k * kk_ref[...]
    k2 = k * (1.0 + (a - 1.0) * ka_ref[...])
    kp = k2 * w_inv
    rk2 = r * k2 * rk_ref[...]
    aw_inv = a * w_inv

    strict2, causal2 = _tri_masks(tb, 2)
    units = [(j, h) for j in range(sb) for h in range(nh)]

    def blk(x, j, h):
        return x[j * tb:(j + 1) * tb, h * R_HEAD:(h + 1) * R_HEAD]

    xs_, ys_, vs_ = [], [], []
    for j, h in units:
        kx_u = blk(kx, j, h)
        kk_u = kx_u * lax.rsqrt(jnp.sum(kx_u * kx_u, axis=-1, keepdims=True) + 1e-6)
        ap = -kk_u * blk(w_exc, j, h)
        bp = kk_u * blk(aw_inv, j, h)
        xs_.append(jnp.concatenate([ap, blk(rp, j, h)], axis=0))
        ys_.append(jnp.concatenate([bp, blk(kp, j, h)], axis=0))
        vs_.append(blk(v, j, h))
    s_old = [s_ref[j, h] for j, h in units]
    amats = [_mm(x, y, NT, PASSES_MAIN) for x, y in zip(xs_, ys_)]
    xss = [_mm(x, s, NT, PASSES_MAIN) for x, s in zip(xs_, s_old)]
    tops = [jnp.where(strict2, am[:tb], 0.0) for am in amats]
    bots = [jnp.where(causal2, am[tb:], 0.0) for am in amats]
    tinvs = _inv_unit_lower([tp[:, :tb] for tp in tops], levels, R_PASSES_INV)
    rhs = [xs[:tb] + _mm(tp[:, tb:], vv, NN, PASSES_MAIN) for xs, tp, vv in zip(xss, tops, vs_)]
    us = [_mm(t, rr, NN, PASSES_MAIN) for t, rr in zip(tinvs, rhs)]
    uvs = [jnp.concatenate([u, vv], axis=0) for u, vv in zip(us, vs_)]
    os_ = [xs[tb:] + _mm(bt, uv, NN, PASSES_MAIN) for xs, bt, uv in zip(xss, bots, uvs)]
    upd = [_mm(uv, y, TN, PASSES_MAIN) for uv, y in zip(uvs, ys_)]
    for i, (j, h) in enumerate(units):
        s_ref[j, h] = (s_old[i] + upd[i]) * blk(w_inc, j, h)[tb - 1:tb, :]

    lnw = lw_ref[...]
    lnb = lb_ref[...]
    out_rows = []
    for j in range(sb):
        outs = []
        for h in range(nh):
            o = os_[j * nh + h]
            mu = jnp.mean(o, axis=-1, keepdims=True)
            var = jnp.mean(jnp.square(o - mu), axis=-1, keepdims=True)
            sl = slice(h * R_HEAD, (h + 1) * R_HEAD)
            on = (o - mu) * lax.rsqrt(var + RWKV_GN_EPS) * lnw[:, sl] + lnb[:, sl]
            bonus = jnp.sum(blk(rk2, j, h), axis=-1, keepdims=True) * vs_[j * nh + h]
            outs.append(on + bonus)
        out_rows.append(jnp.concatenate(outs, axis=1))
    out = out_rows[0] if sb == 1 else jnp.concatenate(out_rows, axis=0)
    ya_ref[...] = (out * g).reshape(sb, tb, HBW).astype(ya_ref.dtype)

    @pl.when(ci == pl.num_programs(2) - 1)
    def _():
        sout_ref[...] = s_ref[...]


def _seq_block(nb, t_pad, nc, long_seq_block):
    sb = long_seq_block if nc > 1 else max(1, SEQ_BLOCK_ROWS // t_pad)
    while nb % sb:
        sb -= 1
    return sb


def _rwkv(p_all, nb, t_pad, t_valid, rw, prm, state):
    tb = min(CHUNK, t_pad)
    nc = t_pad // tb
    sb = _seq_block(nb, t_pad, nc, R_LONG_SEQ_BLOCK)
    nhb = rw // HBW
    heads = rw // R_HEAD
    kb, vb, lb = rw // HBW, 2 * rw // HBW, 3 * rw // HBW
    has_state = state is not None
    levels = max(1, (min(tb, t_valid or tb) - 1).bit_length())
    ya_dtype = BF16 if tb % (2 * SUBLANES) == 0 else F32

    def pcol(off):
        return pl.BlockSpec((sb, tb, HBW), lambda b, h, ci: (b, ci, off + h))

    def vec(off):
        return pl.BlockSpec((1, HBW), lambda b, h, ci: (0, off + h))

    in_specs = [pcol(0), pcol(kb), pcol(vb),
                pl.BlockSpec((sb, tb, HBW), lambda b, h, ci: (b, ci, lb)),
                vec(0), vec(kb), vec(vb),
                pl.BlockSpec((1, HBW), lambda b, h, ci: (0, lb)),
                vec(0), vec(0), vec(0), vec(0), vec(0), vec(0), vec(0),
                pl.BlockSpec((prm["ww2"].shape[0], HBW), lambda b, h, ci: (0, h)),
                pl.BlockSpec((prm["wa2"].shape[0], HBW), lambda b, h, ci: (0, h)),
                pl.BlockSpec((prm["wg2"].shape[0], HBW), lambda b, h, ci: (0, h))]
    args = [p_all, p_all, p_all, p_all, prm["mu"], prm["mu"], prm["mu"], prm["mu"],
            prm["w0"], prm["a0"], prm["k_k"], prm["k_a"], prm["r_k"], prm["lnx_w"], prm["lnx_b"],
            prm["ww2"], prm["wa2"], prm["wg2"]]
    state_spec = pl.BlockSpec((sb, R_HB, R_HEAD, R_HEAD), lambda b, h, ci: (b, h, 0, 0))
    if has_state:
        s0, shift = state
        in_specs += [state_spec,
                     pl.BlockSpec((sb, 1, HBW), lambda b, h, ci: (b, 0, h)),
                     pl.BlockSpec((sb, 1, HBW), lambda b, h, ci: (b, 0, kb + h)),
                     pl.BlockSpec((sb, 1, HBW), lambda b, h, ci: (b, 0, vb + h)),
                     pl.BlockSpec((sb, 1, HBW), lambda b, h, ci: (b, 0, lb))]
        args += [s0, shift, shift, shift, shift]
    kern = functools.partial(_rwkv_kernel, tb=tb, sb=sb, t_valid=t_valid, has_state=has_state,
                             levels=levels)
    return pl.pallas_call(
        kern,
        grid=(nb // sb, nhb, nc),
        in_specs=in_specs,
        out_specs=[pl.BlockSpec((sb, tb, HBW), lambda b, h, ci: (b, ci, h)), state_spec],
        out_shape=[jax.ShapeDtypeStruct((nb, t_pad, rw), ya_dtype),
                   jax.ShapeDtypeStruct((nb, heads, R_HEAD, R_HEAD), F32)],
        scratch_shapes=[pltpu.VMEM((sb, R_HB, R_HEAD, R_HEAD), F32),
                        pltpu.VMEM((4, sb, SUBLANES, HBW), F32)],
        compiler_params=_cparams(("parallel", "parallel", "arbitrary")),
        name="rwkv7_chunk",
    )(*args)


def _gdn_kernel(*refs, tb, sb, t_valid, has_state, levels):
    (q_ref, k_ref, v_ref, z_ref, ba_ref, cq_ref, ck_ref, cv_ref,
     an_ref, dt_ref, nw_ref) = refs[:11]
    n = 11
    if has_state:
        s0_ref, hq_ref, hk_ref, hv_ref = refs[n:n + 4]
        n += 4
    yb_ref, sout_ref, s_ref, hist_ref = refs[n:n + 4]
    ci = pl.program_id(2)
    rows = sb * tb

    @pl.when(ci == 0)
    def _():
        if has_state:
            s_ref[...] = s0_ref[...]
            hist_ref[0] = hq_ref[...]
            hist_ref[1] = hk_ref[...]
            hist_ref[2] = hv_ref[...]
        else:
            s_ref[...] = jnp.zeros_like(s_ref)
            hist_ref[...] = jnp.zeros_like(hist_ref)

    row = lax.broadcasted_iota(jnp.int32, (rows, 1), 0)

    def conv(x_ref, slot, cw_ref):
        x = x_ref[...].reshape(rows, HBW)
        hists = [hist_ref[slot, j] for j in range(sb)]
        cw = cw_ref[...]
        acc = _shift_rows(x, hists, 3, tb) * cw[0:1, :]
        acc = acc + _shift_rows(x, hists, 2, tb) * cw[1:2, :]
        acc = acc + _shift_rows(x, hists, 1, tb) * cw[2:3, :]
        acc = acc + x * cw[3:4, :]
        for j in range(sb):
            hist_ref[slot, j] = x[(j + 1) * tb - SUBLANES:(j + 1) * tb, :]
        return _silu(acc)

    q = conv(q_ref, 0, cq_ref)
    k = conv(k_ref, 1, ck_ref)
    v = conv(v_ref, 2, cv_ref)
    z = z_ref[...].reshape(rows, HBW)
    ba = ba_ref[...].reshape(rows, LANES)
    beta_all = _sigmoid(ba)
    la_all = an_ref[...] * _softplus(ba + dt_ref[...])
    if t_valid is not None:
        valid = (row % tb < t_valid).astype(F32)
        beta_all = beta_all * valid
        la_all = la_all * valid
        q = q * valid
        k = k * valid
        v = v * valid

    g_all = _mm_exact_lhs(_block_cumsum_matrix(rows, tb), la_all)
    lrow = lax.broadcasted_iota(jnp.int32, (LANES, LANES), 0)
    lcol = lax.broadcasted_iota(jnp.int32, (LANES, LANES), 1)
    eye = jnp.where(lrow == lcol, 1.0, 0.0).astype(BF16)
    g_t = _mm_exact_lhs(eye, g_all, NT)
    eg_all = jnp.exp(g_all)
    strict, causal = _tri_masks(tb)
    nw = nw_ref[...]
    nh = q.shape[1] // G_HEAD
    units = [(j, h) for j in range(sb) for h in range(nh)]

    def blk(x, j, h):
        return x[j * tb:(j + 1) * tb, h * G_HEAD:(h + 1) * G_HEAD]

    qs, ks, kbs, rhs, decays, egs, gcols = [], [], [], [], [], [], []
    for j, h in units:
        rs = slice(j * tb, (j + 1) * tb)
        gcol = g_all[rs, G_HB + h:G_HB + h + 1]
        grow = g_t[G_HB + h:G_HB + h + 1, rs]
        beta = beta_all[rs, h:h + 1]
        eg = eg_all[rs, G_HB + h:G_HB + h + 1]
        decays.append(jnp.where(causal, jnp.exp(jnp.where(causal, gcol - grow, 0.0)), 0.0))
        q_u, k_u, v_u = blk(q, j, h), blk(k, j, h), blk(v, j, h)
        q_u = q_u * lax.rsqrt(jnp.sum(q_u * q_u, axis=-1, keepdims=True) + 1e-6) * (G_HEAD ** -0.5)
        k_u = k_u * lax.rsqrt(jnp.sum(k_u * k_u, axis=-1, keepdims=True) + 1e-6)
        kb = k_u * beta
        qs.append(q_u)
        ks.append(k_u)
        kbs.append(kb)
        rhs.append(v_u * beta)
        egs.append(eg)
        gcols.append(gcol)
    s_old = [s_ref[j, h] for j, h in units]
    amats = [_mm(jnp.concatenate([kb, q_u], axis=0), k_u, NT, G_PASSES_INV)
             for kb, q_u, k_u in zip(kbs, qs, ks)]
    lmats = [jnp.where(strict, am[:tb] * dc, 0.0) for am, dc in zip(amats, decays)]
    attns = [am[tb:] * dc for am, dc in zip(amats, decays)]
    tinvs = _inv_unit_lower([-lm for lm in lmats], levels, G_PASSES_INV)
    xss = [_mm(jnp.concatenate([kb * eg, q_u * eg], axis=0), s, NN, PASSES_MAIN)
           for kb, q_u, eg, s in zip(kbs, qs, egs, s_old)]
    vnews = [_mm(t, rr - xs[:tb], NN, G_PASSES_INV) for t, rr, xs in zip(tinvs, rhs, xss)]
    os_ = [xs[tb:] + _mm(at, vn, NN, PASSES_MAIN) for xs, at, vn in zip(xss, attns, vnews)]
    ktails = [k_u * jnp.exp(gc[tb - 1:tb, :] - gc) for k_u, gc in zip(ks, gcols)]
    upd = [_mm(kt, vn, TN, PASSES_MAIN) for kt, vn in zip(ktails, vnews)]

    out_rows = []
    for j in range(sb):
        outs = []
        for h in range(nh):
            i = j * nh + h
            s_ref[j, h] = s_old[i] * egs[i][tb - 1:tb, :] + upd[i]
            o = os_[i]
            o = o * lax.rsqrt(jnp.mean(o * o, axis=-1, keepdims=True) + NORM_EPS) * nw
            outs.append(o)
        out_rows.append(jnp.concatenate(outs, axis=1))
    out = out_rows[0] if sb == 1 else jnp.concatenate(out_rows, axis=0)
    yb_ref[...] = (out * _silu(z)).reshape(sb, tb, HBW).astype(yb_ref.dtype)

    @pl.when(ci == pl.num_programs(2) - 1)
    def _():
        sout_ref[...] = s_ref[...]


def _gdn(p_all, nb, t_pad, t_valid, gw, goff, baoff, prm, state):
    tb = min(CHUNK, t_pad)
    nc = t_pad // tb
    sb = _seq_block(nb, t_pad, nc, G_LONG_SEQ_BLOCK)
    nhb = gw // HBW
    heads = gw // G_HEAD
    qb = goff // HBW
    kb, vb, zb = qb + gw // HBW, qb + 2 * gw // HBW, qb + 3 * gw // HBW
    bab = baoff // LANES
    has_state = state is not None
    levels = max(1, (min(tb, t_valid or tb) - 1).bit_length())
    yb_dtype = BF16 if tb % (2 * SUBLANES) == 0 else F32

    def pcol(off):
        return pl.BlockSpec((sb, tb, HBW), lambda b, h, ci: (b, ci, off + h))

    def cwspec(off):
        return pl.BlockSpec((G_CONV, HBW), lambda b, h, ci: (0, off + h))

    in_specs = [pcol(qb), pcol(kb), pcol(vb), pcol(zb),
                pl.BlockSpec((sb, tb, LANES), lambda b, h, ci: (b, ci, bab + h)),
                cwspec(0), cwspec(gw // HBW), cwspec(2 * gw // HBW),
                pl.BlockSpec((1, LANES), lambda b, h, ci: (0, h)),
                pl.BlockSpec((1, LANES), lambda b, h, ci: (0, h)),
                pl.BlockSpec((1, G_HEAD), lambda b, h, ci: (0, 0))]
    args = [p_all, p_all, p_all, p_all, p_all, prm["conv_w"], prm["conv_w"], prm["conv_w"],
            prm["neg_a"], prm["dt"], prm["norm_w"]]
    state_spec = pl.BlockSpec((sb, G_HB, G_HEAD, G_HEAD), lambda b, h, ci: (b, h, 0, 0))
    if has_state:
        s0, hist = state
        in_specs += [state_spec,
                     pl.BlockSpec((sb, SUBLANES, HBW), lambda b, h, ci: (b, 0, h)),
                     pl.BlockSpec((sb, SUBLANES, HBW), lambda b, h, ci: (b, 0, gw // HBW + h)),
                     pl.BlockSpec((sb, SUBLANES, HBW), lambda b, h, ci: (b, 0, 2 * gw // HBW + h))]
        args += [s0, hist, hist, hist]
    kern = functools.partial(_gdn_kernel, tb=tb, sb=sb, t_valid=t_valid, has_state=has_state,
                             levels=levels)
    return pl.pallas_call(
        kern,
        grid=(nb // sb, nhb, nc),
        in_specs=in_specs,
        out_specs=[pl.BlockSpec((sb, tb, HBW), lambda b, h, ci: (b, ci, h)), state_spec],
        out_shape=[jax.ShapeDtypeStruct((nb, t_pad, gw), yb_dtype),
                   jax.ShapeDtypeStruct((nb, heads, G_HEAD, G_HEAD), F32)],
        scratch_shapes=[pltpu.VMEM((sb, G_HB, G_HEAD, G_HEAD), F32),
                        pltpu.VMEM((3, sb, SUBLANES, HBW), F32)],
        compiler_params=_cparams(("parallel", "parallel", "arbitrary")),
        name="gdn_chunk",
    )(*args)


def _run_group(x, mods, nb, t, lay, wts, rprm, gprm, states):
    m, d = x.shape
    rw, gw = lay["rw"], lay["gw"]
    row_tiles = (1024, 512, 256, 128, 64, 32, 16, 8)
    span = t if mods.arr.ndim == 3 else m
    tm = _pick(span, row_tiles[1:])
    tm_proj = _pick(span, row_tiles)
    n_all = wts["w_all"].shape[1]
    tm_norm = min(tm, 256)

    h = _norm_mod(x, wts["norm1_w"], mods, 0, 1, tm_norm)
    p_all = _matmul(h, wts["w_all"], tm_proj, _pick(n_all, (1024, 512, 256, 128)), F32, "in_proj")

    p3 = p_all.reshape(nb, t, n_all)
    shift_new = p3[:, t - 1, :lay["r_cols"]]
    conv_new = p3[:, t - (G_CONV - 1):, lay["goff"]:lay["goff"] + 3 * gw]

    if t % SUBLANES == 0:
        t_pad, t_valid, p_rec = t, None, p3
    else:
        t_pad = -(-t // SUBLANES) * SUBLANES
        t_valid = t
        p_rec = jnp.pad(p3, ((0, 0), (0, t_pad - t), (0, 0)))

    ya, wkv_new = _rwkv(p_rec, nb, t_pad, t_valid, rw, rprm, states and states["rwkv"])
    yb, gdn_new = _gdn(p_rec, nb, t_pad, t_valid, gw, lay["goff"], lay["baoff"], gprm,
                       states and states["gdn"])
    ya = ya[:, :t].reshape(m, rw).astype(BF16)
    yb = yb[:, :t].reshape(m, gw).astype(BF16)

    tn_m = _pick(d, (1024, 512, 256, 128))
    while lay["moff"] % tn_m:
        tn_m //= 2
    merged = _merge(ya, yb, wts["w_out_a"], wts["w_out_b"], p_all, lay["moff"], tm, tn_m)
    x1 = _out_resid(merged, wts["w_out"], x, mods, 2, tm, _pick(d, (1024, 512, 256, 128)))
    h2 = _norm_mod(x1, wts["norm2_w"], mods, 3, 4, tm_norm)
    dff = wts["w_up"].shape[1]
    up = _up_proj(h2, wts["w_up"], tm_proj, _pick(dff, (1024, 512, 256, 128)))
    y = _down_final(up, wts["w_down"], x1, mods, 5, wts["final_norm_w"], tm,
                    _pick(dff, (1024, 512, 256, 128)))
    return y, wkv_new, shift_new, gdn_new, conv_new


def kernel(x_prompt, x_sample, state_rwkv_wkv, state_rwkv_shift, state_gdn, state_gdn_conv,
           c_prompt, c_sample, norm1_w, norm2_w, w_ada, b_ada, w_in, r_mu, r_w0, r_w_w2, r_a0,
           r_w_a2, r_w_g2, r_k_k, r_k_a, r_r_k, r_lnx_w, r_lnx_b, g_conv_w, g_a_log, g_dt_bias,
           g_norm_w, w_out_a, w_out_b, w_out, w_up, w_down, final_norm_w):
    depth = w_in.shape[0]
    assert depth == 1, "single-layer trunk"
    bp, tp, d = x_prompt.shape
    bs, ts, _ = x_sample.shape
    rw = r_w0.shape[1]
    gw = g_conv_w.shape[2] // 3
    g_heads = g_a_log.shape[1]
    dlw, dla, dlg = r_w_w2.shape[1], r_w_a2.shape[1], r_w_g2.shape[1]
    lora = dlw + dla + dlg
    r_cols = 3 * rw + lora
    g_cols = 4 * gw + 2 * g_heads
    assert rw % HBW == 0 and gw % HBW == 0 and lora <= HBW and g_heads % G_HB == 0
    assert w_in.shape[2] == r_cols + g_cols + 2 * d

    rseg = 3 * rw + HBW
    goff = rseg
    baoff = goff + 4 * gw
    n_ba = (g_heads // G_HB) * LANES
    moff = baoff + n_ba
    lay = dict(rw=rw, gw=gw, r_cols=r_cols, goff=goff, baoff=baoff, moff=moff)

    wi = w_in[0]
    w_b = wi[:, r_cols + 4 * gw:r_cols + 4 * gw + g_heads].reshape(d, g_heads // G_HB, G_HB)
    w_a = wi[:, r_cols + 4 * gw + g_heads:r_cols + g_cols].reshape(d, g_heads // G_HB, G_HB)
    w_ba = jnp.concatenate([w_b, w_a, jnp.zeros((d, g_heads // G_HB, LANES - 2 * G_HB), F32)],
                           axis=2).reshape(d, n_ba)
    w_all = jnp.concatenate([wi[:, :r_cols], jnp.zeros((d, rseg - r_cols), F32),
                             wi[:, r_cols:r_cols + 4 * gw], w_ba,
                             wi[:, r_cols + g_cols:]], axis=1).astype(BF16)

    wts = dict(w_all=w_all, norm1_w=norm1_w[0], norm2_w=norm2_w[0],
               w_out_a=w_out_a[0].astype(BF16), w_out_b=w_out_b[0].astype(BF16),
               w_out=w_out[0].astype(BF16), w_up=w_up[0].astype(BF16),
               w_down=w_down[0].astype(BF16), final_norm_w=final_norm_w)

    row = lambda a: a.reshape(1, -1)

    def hilo(w):
        hi = w.astype(BF16)
        return jnp.concatenate([hi, (w - hi.astype(F32)).astype(BF16)], axis=0)

    rprm = dict(mu=jnp.pad(row(r_mu[0]), ((0, 0), (0, rseg - r_cols))),
                w0=row(r_w0[0]), a0=row(r_a0[0]), k_k=row(r_k_k[0]), k_a=row(r_k_a[0]),
                r_k=row(r_r_k[0]), lnx_w=row(r_lnx_w[0]), lnx_b=row(r_lnx_b[0]),
                ww2=hilo(r_w_w2[0]), wa2=hilo(r_w_a2[0]), wg2=hilo(r_w_g2[0]))

    def ba_row(vals):
        v = vals.reshape(g_heads // G_HB, G_HB)
        z = jnp.zeros_like(v)
        return jnp.concatenate([z, v, jnp.zeros((g_heads // G_HB, LANES - 2 * G_HB), F32)],
                               axis=1).reshape(1, n_ba)

    gprm = dict(conv_w=g_conv_w[0], neg_a=ba_row(-jnp.exp(g_a_log[0])), dt=ba_row(g_dt_bias[0]),
                norm_w=row(g_norm_w[0]))

    c_all = jnp.concatenate([c_prompt, c_sample], axis=0)
    n_seq = bp + bs
    c_all = jnp.pad(c_all, ((0, -n_seq % SUBLANES), (0, 0)))
    mods_all = _adaln(c_all, w_ada[0], b_ada[0])
    mods_p = _Mods(mods_all[:bp].reshape(bp, 1, 6 * d), d, tp)
    mods_s = _Mods(mods_all[bp:n_seq], d, ts)

    y_p, p_wkv, p_shift, p_gdn, p_conv = _run_group(
        x_prompt.reshape(bp * tp, d), mods_p, bp, tp, lay, wts, rprm, gprm, None)

    shift0 = jnp.pad(state_rwkv_shift[0], ((0, 0), (0, rseg - r_cols))).reshape(bs, 1, rseg)
    hist0 = jnp.pad(state_gdn_conv[0], ((0, 0), (SUBLANES - (G_CONV - 1), 0), (0, 0)))
    states = dict(rwkv=(state_rwkv_wkv[0], shift0), gdn=(state_gdn[0], hist0))
    y_s, s_wkv, s_shift, s_gdn, s_conv = _run_group(
        x_sample.reshape(bs * ts, d), mods_s, bs, ts, lay, wts, rprm, gprm, states)

    return (y_p.reshape(bp, tp, d), y_s.reshape(bs, ts, d),
            p_wkv[None], p_shift[None], p_gdn[None], p_conv[None],
            s_wkv[None], s_shift[None], s_gdn[None], s_conv[None])
```
